```python
import jax, jax.numpy as jnp
from jax import lax
import numpy as np

D_MODEL = 2048
BATCH = 4
SEQ = 4096
DEPTH = 2

HEAD_DIM = 128
EPS = 1e-6
A_HEADS = 8
A_KV_HEADS = 2
A_GROUP = A_HEADS // A_KV_HEADS
A_WINDOW = 128
A_BLOCK = 128
B_HEADS = 4
GRID_W = 64
NB_ROWS_MAX = 8
NB_COLS = 16
C_HEADS = 4
C_Q_RANK = 512
C_KV_RANK = 256
C_NOPE = 128
C_ROPE = 64
C_V = 128
C_BLOCK = 128
ROPE_THETA = 10000.0
A_Q = A_HEADS * HEAD_DIM
A_KV = A_KV_HEADS * HEAD_DIM
B_W = B_HEADS * HEAD_DIM
C_OUT = C_HEADS * C_V
D_MIX = A_Q + B_W + C_OUT
IN_SPLITS = (A_Q, A_KV, A_KV, B_W, B_W, B_W, C_Q_RANK, C_KV_RANK, C_ROPE)
D_IN = A_Q + 2 * A_KV + 3 * B_W + C_Q_RANK + C_KV_RANK + C_ROPE
PEER_HEADS = 8
PEER_KEYS = 128
PEER_EXPERTS = PEER_KEYS * PEER_KEYS
PEER_KEY_DIM = 128
PEER_TOPK = 16
PEER_CHUNK = 128

kernel_name = 'hybrid_parallel_heads_peer_encoder'


def rms_norm(x, g):
    xf = x.astype(jnp.float32)
    y = xf * lax.rsqrt(jnp.mean(xf * xf, axis=-1, keepdims=True) + EPS)
    return (y * g.astype(jnp.float32)).astype(x.dtype)


def alibi_slopes(n):
    return jnp.asarray([2.0 ** (-8.0 * (h + 1) / n) for h in range(n)], dtype=jnp.float32)


def apply_rope(x):
    s_len, d = x.shape[1], x.shape[-1]
    half = d // 2
    inv = ROPE_THETA ** (-jnp.arange(half, dtype=jnp.float32) / half)
    ang = jnp.arange(s_len, dtype=jnp.float32)[:, None] * inv[None, :]
    cos = jnp.cos(ang)[None, :, None, :]
    sin = jnp.sin(ang)[None, :, None, :]
    x1 = x[..., :half].astype(jnp.float32)
    x2 = x[..., half:].astype(jnp.float32)
    return jnp.concatenate([x1 * cos - x2 * sin, x1 * sin + x2 * cos], axis=-1).astype(x.dtype)


def window_gqa_attention(q, k, v, sink):
    bsz, s_len = q.shape[0], q.shape[1]
    nb = s_len // A_BLOCK
    qb = q.reshape(bsz, nb, A_BLOCK, A_KV_HEADS, A_GROUP, HEAD_DIM)

    def band(t):
        tb = t.reshape(bsz, nb, A_BLOCK, A_KV_HEADS, HEAD_DIM)
        tp = jnp.pad(tb, ((0, 0), (1, 1), (0, 0), (0, 0), (0, 0)))
        return jnp.concatenate([tp[:, :-2], tp[:, 1:-1], tp[:, 2:]], axis=2)

    kb, vb = band(k), band(v)
    s = jnp.einsum('bnqkgd,bnskd->bnkgqs', qb, kb).astype(jnp.float32) * (HEAD_DIM ** -0.5)
    blk = jnp.arange(nb)[:, None, None] * A_BLOCK
    q_pos = blk + jnp.arange(A_BLOCK)[None, :, None]
    k_pos = blk + jnp.arange(3 * A_BLOCK)[None, None, :] - A_BLOCK
    dist = jnp.abs(q_pos - k_pos)
    valid = (dist <= A_WINDOW) & (k_pos >= 0) & (k_pos < s_len)
    slopes = alibi_slopes(A_HEADS).reshape(A_KV_HEADS, A_GROUP)
    s = s - slopes[None, None, :, :, None, None] * dist[None, :, None, None].astype(jnp.float32)
    s = jnp.where(valid[None, :, None, None], s, -jnp.inf)
    sk = sink.astype(jnp.float32).reshape(A_KV_HEADS, A_GROUP)[None, None, :, :, None, None]
    m = jnp.maximum(jnp.max(s, axis=-1, keepdims=True), sk)
    p = jnp.exp(s - m)
    p = p / (jnp.sum(p, axis=-1, keepdims=True) + jnp.exp(sk - m))
    o = jnp.einsum('bnkgqs,bnskd->bnqkgd', p.astype(v.dtype), vb)
    return o.reshape(bsz, s_len, A_Q)


def neighbourhood_attention(q, k, v, rel_bias):
    bsz, s_len = q.shape[0], q.shape[1]
    rows = s_len // GRID_W
    kr = min(NB_ROWS_MAX, rows)
    qg = jnp.moveaxis(q.reshape(bsz, rows, GRID_W, B_HEADS, HEAD_DIM), 1, 0)
    kg = k.reshape(bsz, rows, GRID_W, B_HEADS, HEAD_DIM)
    vg = v.reshape(bsz, rows, GRID_W, B_HEADS, HEAD_DIM)
    col = jnp.arange(GRID_W)
    c_start = jnp.clip(col - NB_COLS // 2, 0, GRID_W - NB_COLS)
    col_ok = (col[None, :] >= c_start[:, None]) & (col[None, :] < c_start[:, None] + NB_COLS)
    dc_idx = jnp.clip(col[None, :] - col[:, None] + NB_COLS - 1, 0, 2 * NB_COLS - 2)
    scale = HEAD_DIM ** -0.5

    def one_row(args):
        r, q_r = args
        r_start = jnp.clip(r - kr // 2, 0, rows - kr)
        k_r = lax.dynamic_slice_in_dim(kg, r_start, kr, axis=1)
        v_r = lax.dynamic_slice_in_dim(vg, r_start, kr, axis=1)
        dr_idx = r_start + jnp.arange(kr) - r + NB_ROWS_MAX - 1
        bias = rel_bias[:, dr_idx[None, :, None], dc_idx[:, None, :]].astype(jnp.float32)
        bias = jnp.where(col_ok[None, :, None, :], bias, -jnp.inf)
        s = jnp.einsum('bqhd,brkhd->bhqrk', q_r, k_r).astype(jnp.float32) * scale + bias[None]
        p = jax.nn.softmax(s.reshape(bsz, B_HEADS, GRID_W, kr * GRID_W), axis=-1)
        p = p.reshape(bsz, B_HEADS, GRID_W, kr, GRID_W).astype(v_r.dtype)
        return jnp.einsum('bhqrk,brkhd->bqhd', p, v_r)

    o = lax.map(one_row, (jnp.arange(rows), qg))
    return jnp.moveaxis(o, 0, 1).reshape(bsz, s_len, B_W)


def mla_attention(c_q, c_kv, k_rope, g_cq, g_ckv, w_uq, w_ukv):
    bsz, s_len = c_q.shape[0], c_q.shape[1]
    q = (rms_norm(c_q, g_cq) @ w_uq).reshape(bsz, s_len, C_HEADS, C_NOPE + C_ROPE)
    q_nope = q[..., :C_NOPE]
    q_pe = apply_rope(q[..., C_NOPE:])
    kv = (rms_norm(c_kv, g_ckv) @ w_ukv).reshape(bsz, s_len, C_HEADS, C_NOPE + C_V)
    k_nope = kv[..., :C_NOPE]
    v = kv[..., C_NOPE:]
    k_pe = apply_rope(k_rope[:, :, None, :])[:, :, 0]
    nb = s_len // C_BLOCK
    scale = (C_NOPE + C_ROPE) ** -0.5

    def to_blocks(t):
        return jnp.moveaxis(t.reshape(bsz, nb, C_BLOCK, *t.shape[2:]), 1, 0)

    def one_block(args):
        qn, qp = args
        s = (jnp.einsum('bqhd,bkhd->bhqk', qn, k_nope)
             + jnp.einsum('bqhd,bkd->bhqk', qp, k_pe)).astype(jnp.float32) * scale
        p = jax.nn.softmax(s, axis=-1).astype(v.dtype)
        return jnp.einsum('bhqk,bkhd->bqhd', p, v)

    o = lax.map(one_block, (to_blocks(q_nope), to_blocks(q_pe)))
    return jnp.moveaxis(o, 0, 1).reshape(bsz, s_len, C_OUT)


def hybrid_mixer(xn, w_in, a_sink, b_rel_bias, c_q_norm, c_kv_norm, c_w_uq, c_w_ukv, out_norm, w_o):
    bsz, s_len, _ = xn.shape
    h = xn @ w_in
    cuts = [int(c) for c in np.cumsum(IN_SPLITS)[:-1]]
    qa, ka, va, qb, kb, vb, cq, ckv, kr = jnp.split(h, cuts, axis=-1)
    o_a = window_gqa_attention(qa.reshape(bsz, s_len, A_HEADS, HEAD_DIM),
                               ka.reshape(bsz, s_len, A_KV_HEADS, HEAD_DIM),
                               va.reshape(bsz, s_len, A_KV_HEADS, HEAD_DIM), a_sink)
    o_b = neighbourhood_attention(qb.reshape(bsz, s_len, B_HEADS, HEAD_DIM),
                                  kb.reshape(bsz, s_len, B_HEADS, HEAD_DIM),
                                  vb.reshape(bsz, s_len, B_HEADS, HEAD_DIM), b_rel_bias)
    o_c = mla_attention(cq, ckv, kr, c_q_norm, c_kv_norm, c_w_uq, c_w_ukv)
    o = jnp.concatenate([rms_norm(o_a, out_norm[:A_Q]),
                         rms_norm(o_b, out_norm[A_Q:A_Q + B_W]),
                         rms_norm(o_c, out_norm[A_Q + B_W:])], axis=-1)
    return o @ w_o


def peer_ffn(x, w_q, sub_keys, u, vv):
    bsz, s_len, d = x.shape
    q = (x @ w_q).reshape(bsz, s_len, PEER_HEADS, 2, PEER_KEY_DIM // 2).astype(jnp.float32)
    s = jnp.einsum('bshcd,cnd->bshcn', q, sub_keys.astype(jnp.float32))
    top_s, top_i = lax.top_k(s, PEER_TOPK)
    cand_s = (top_s[..., 0, :, None] + top_s[..., 1, None, :]).reshape(bsz, s_len, PEER_HEADS, PEER_TOPK * PEER_TOPK)
    cand_i = (top_i[..., 0, :, None] * PEER_KEYS + top_i[..., 1, None, :]).reshape(bsz, s_len, PEER_HEADS, PEER_TOPK * PEER_TOPK)
    best_s, best_j = lax.top_k(cand_s, PEER_TOPK)
    idx = jnp.take_along_axis(cand_i, best_j, axis=-1)
    g = jax.nn.softmax(best_s, axis=-1).astype(x.dtype)
    n_tok = bsz * s_len
    n_sel = PEER_HEADS * PEER_TOPK
    xc = x.reshape(n_tok // PEER_CHUNK, PEER_CHUNK, d)
    ic = idx.reshape(n_tok // PEER_CHUNK, PEER_CHUNK, n_sel)
    gc = g.reshape(n_tok // PEER_CHUNK, PEER_CHUNK, n_sel)

    def chunk(args):
        xt, it, gt = args
        a = jnp.einsum('td,tkd->tk', xt, u[it])
        hsel = jax.nn.gelu(a) * gt
        return jnp.einsum('tk,tkd->td', hsel, vv[it])

    y = lax.map(chunk, (xc, ic, gc))
    return y.reshape(bsz, s_len, d)


def setup_inputs(seed: int = 0) -> dict:
    key = jax.random.key(seed)
    ks = jax.random.split(key, 17)
    f32 = jnp.float32

    def nrm(k, shape, scale):
        return jax.random.normal(k, shape, f32) * scale

    def gain(k, shape):
        return 1.0 + 0.05 * jax.random.normal(k, shape, f32)

    return {
        'x': nrm(ks[0], (BATCH, SEQ, D_MODEL), 1.0),
        'ln1': gain(ks[1], (DEPTH, D_MODEL)),
        'w_in': nrm(ks[2], (DEPTH, D_MODEL, D_IN), D_MODEL ** -0.5),
        'a_sink': nrm(ks[3], (DEPTH, A_HEADS), 1.0),
        'b_rel_bias': nrm(ks[4], (DEPTH, B_HEADS, 2 * NB_ROWS_MAX - 1, 2 * NB_COLS - 1), 0.5),
        'c_q_norm': gain(ks[5], (DEPTH, C_Q_RANK)),
        'c_kv_norm': gain(ks[6], (DEPTH, C_KV_RANK)),
        'c_w_uq': nrm(ks[7], (DEPTH, C_Q_RANK, C_HEADS * (C_NOPE + C_ROPE)), C_Q_RANK ** -0.5),
        'c_w_ukv': nrm(ks[8], (DEPTH, C_KV_RANK, C_HEADS * (C_NOPE + C_V)), C_KV_RANK ** -0.5),
        'out_norm': gain(ks[9], (DEPTH, D_MIX)),
        'w_o': nrm(ks[10], (DEPTH, D_MIX, D_MODEL), D_MIX ** -0.5),
        'ln2': gain(ks[11], (DEPTH, D_MODEL)),
        'peer_w_q': nrm(ks[12], (DEPTH, D_MODEL, PEER_HEADS * PEER_KEY_DIM), D_MODEL ** -0.5),
        'peer_sub_keys': nrm(ks[13], (DEPTH, 2, PEER_KEYS, PEER_KEY_DIM // 2), (PEER_KEY_DIM // 2) ** -0.5),
        'peer_u': nrm(ks[14], (DEPTH, PEER_EXPERTS, D_MODEL), D_MODEL ** -0.5),
        'peer_v': nrm(ks[15], (DEPTH, PEER_EXPERTS, D_MODEL), PEER_TOPK ** -0.5),
        'final_norm': gain(ks[16], (D_MODEL,)),
    }


def reference(x, ln1, w_in, a_sink, b_rel_bias, c_q_norm, c_kv_norm, c_w_uq, c_w_ukv, out_norm, w_o,
              ln2, peer_w_q, peer_sub_keys, peer_u, peer_v, final_norm):
    for l in range(DEPTH):
        x = x + hybrid_mixer(rms_norm(x, ln1[l]), w_in[l], a_sink[l], b_rel_bias[l], c_q_norm[l],
                             c_kv_norm[l], c_w_uq[l], c_w_ukv[l], out_norm[l], w_o[l])
        x = x + peer_ffn(rms_norm(x, ln2[l]), peer_w_q[l], peer_sub_keys[l], peer_u[l], peer_v[l])
    return rms_norm(x, final_norm)
```

```python
import functools
import math

import jax
import jax.numpy as jnp
import numpy as np
from jax import lax
from jax.experimental import pallas as pl
from jax.experimental.pallas import tpu as pltpu

F32 = jnp.float32
BF16 = jnp.bfloat16

EPS = 1e-6
HEAD_DIM = 128
A_HEADS = 8
A_KV_HEADS = 2
A_GROUP = A_HEADS // A_KV_HEADS
A_WINDOW = 128
B_HEADS = 4
GRID_W = 64
NB_ROWS = 8
NB_COLS = 16
C_HEADS = 4
C_Q_RANK = 512
C_KV_RANK = 256
C_NOPE = 128
C_ROPE = 64
C_V = 128
ROPE_THETA = 10000.0
A_Q = A_HEADS * HEAD_DIM
A_KV = A_KV_HEADS * HEAD_DIM
B_W = B_HEADS * HEAD_DIM
C_OUT = C_HEADS * C_V
D_MIX = A_Q + B_W + C_OUT
D_IN = A_Q + 2 * A_KV + 3 * B_W + C_Q_RANK + C_KV_RANK + C_ROPE
D_IN_PAD = 4096
KR_OFF = D_IN - C_ROPE
PEER_HEADS = 8
PEER_KEYS = 128
PEER_HALF = 64
PEER_TOPK = 16

VMEM_LIMIT = 56 * 1024 * 1024
NEG_INF = float("-inf")


def _cparams(sem):
    return pltpu.CompilerParams(dimension_semantics=sem, vmem_limit_bytes=VMEM_LIMIT)


def _rms(x, g):
    ms = jnp.mean(x * x, axis=-1, keepdims=True)
    return x * lax.rsqrt(ms + EPS) * g


def _norm_matmul_kernel(x_ref, g_ref, w_ref, o_ref, xn_ref):
    @pl.when(pl.program_id(1) == 0)
    def _():
        xn_ref[...] = _rms(x_ref[...], g_ref[...]).astype(BF16)

    o_ref[...] = jnp.dot(xn_ref[...], w_ref[...],
                         preferred_element_type=F32).astype(o_ref.dtype)


def _norm_matmul(x, g, w, tm=1024, tn=512):
    n, d = x.shape
    nout = w.shape[1]
    tm = min(tm, n)
    return pl.pallas_call(
        _norm_matmul_kernel,
        grid=(n // tm, nout // tn),
        in_specs=[pl.BlockSpec((tm, d), lambda i, j: (i, 0)),
                  pl.BlockSpec((1, d), lambda i, j: (0, 0)),
                  pl.BlockSpec((d, tn), lambda i, j: (0, j))],
        out_specs=pl.BlockSpec((tm, tn), lambda i, j: (i, j)),
        out_shape=jax.ShapeDtypeStruct((n, nout), BF16),
        scratch_shapes=[pltpu.VMEM((tm, d), BF16)],
        compiler_params=_cparams(("parallel", "arbitrary")),
        name="norm_matmul",
    )(x, g.reshape(1, d), w)


A_TQ = 512
A_BAND = 3 * A_WINDOW


def _attn_a_kernel(sink_ref, q_ref, k_ref, v_ref, g_ref, o_ref):
    i = pl.program_id(1)
    s_len = k_ref.shape[0]
    scale = HEAD_DIM ** -0.5
    slopes = [2.0 ** (-8.0 * (h + 1) / A_HEADS) for h in range(A_HEADS)]
    for j in range(A_TQ // A_WINDOW):
        q0 = i * A_TQ + j * A_WINDOW
        start = pl.multiple_of(jnp.clip(q0 - A_WINDOW, 0, s_len - A_BAND), A_WINDOW)
        kb = k_ref[pl.ds(start, A_BAND), :]
        vb = v_ref[pl.ds(start, A_BAND), :]
        q_pos = q0 + lax.broadcasted_iota(jnp.int32, (A_WINDOW, A_BAND), 0)
        k_pos = start + lax.broadcasted_iota(jnp.int32, (A_WINDOW, A_BAND), 1)
        dist = jnp.abs(q_pos - k_pos)
        valid = dist <= A_WINDOW
        dist_f = dist.astype(F32)
        outs = []
        ssq = jnp.zeros((A_WINDOW, 1), F32)
        for kh in range(A_KV_HEADS):
            kk = kb[:, kh * HEAD_DIM:(kh + 1) * HEAD_DIM]
            vv = vb[:, kh * HEAD_DIM:(kh + 1) * HEAD_DIM]
            for g in range(A_GROUP):
                h = kh * A_GROUP + g
                qh = q_ref[j * A_WINDOW:(j + 1) * A_WINDOW, h * HEAD_DIM:(h + 1) * HEAD_DIM]
                s = lax.dot_general(qh, kk, (((1,), (1,)), ((), ())),
                                    preferred_element_type=F32) * scale
                s = s - slopes[h] * dist_f
                s = jnp.where(valid, s, NEG_INF)
                sink = sink_ref[h]
                m = jnp.maximum(jnp.max(s, axis=-1, keepdims=True), sink)
                p = jnp.exp(s - m)
                denom = jnp.sum(p, axis=-1, keepdims=True) + jnp.exp(sink - m)
                p = p / denom
                o = jnp.dot(p.astype(BF16), vv, preferred_element_type=F32)
                ssq = ssq + jnp.sum(o * o, axis=-1, keepdims=True)
                outs.append(o)
        inv = lax.rsqrt(ssq / A_Q + EPS)
        for h in range(A_HEADS):
            cols = slice(h * HEAD_DIM, (h + 1) * HEAD_DIM)
            o_ref[j * A_WINDOW:(j + 1) * A_WINDOW, cols] = (
                outs[h] * inv * g_ref[:, cols]).astype(o_ref.dtype)


def _attn_a(h3, sink, gain):
    bsz, s_len, _ = h3.shape
    assert s_len % A_TQ == 0 and s_len >= A_BAND
    kcol = A_Q // A_KV
    return pl.pallas_call(
        _attn_a_kernel,
        grid=(bsz, s_len // A_TQ),
        in_specs=[pl.BlockSpec(memory_space=pltpu.SMEM),
                  pl.BlockSpec((None, A_TQ, A_Q), lambda b, i: (b, i, 0)),
                  pl.BlockSpec((None, s_len, A_KV), lambda b, i: (b, 0, kcol)),
                  pl.BlockSpec((None, s_len, A_KV), lambda b, i: (b, 0, kcol + 1)),
                  pl.BlockSpec((1, A_Q), lambda b, i: (0, 0))],
        out_specs=pl.BlockSpec((None, A_TQ, A_Q), lambda b, i: (b, i, 0)),
        out_shape=jax.ShapeDtypeStruct((bsz, s_len, A_Q), BF16),
        compiler_params=_cparams(("parallel", "arbitrary")),
        name="attn_a",
    )(sink, h3, h3, h3, gain.reshape(1, A_Q))


B_RQ = 8
B_KEYS = NB_ROWS * GRID_W
B_QCOL = (A_Q + 2 * A_KV) // B_W


def _attn_b_kernel(q_ref, k_ref, v_ref, bias_ref, g_ref, o_ref):
    i = pl.program_id(1)
    rows = k_ref.shape[0] // GRID_W
    scale = HEAD_DIM ** -0.5
    for jr in range(B_RQ):
        r = i * B_RQ + jr
        r_start = jnp.clip(r - NB_ROWS // 2, 0, rows - NB_ROWS)
        off = r_start - r + NB_ROWS - 1
        kstart = pl.multiple_of(r_start * GRID_W, GRID_W)
        kb = k_ref[pl.ds(kstart, B_KEYS), :]
        vb = v_ref[pl.ds(kstart, B_KEYS), :]
        outs = []
        ssq = jnp.zeros((GRID_W, 1), F32)
        for h in range(B_HEADS):
            cols = slice(h * HEAD_DIM, (h + 1) * HEAD_DIM)
            qh = q_ref[jr * GRID_W:(jr + 1) * GRID_W, cols]
            s = lax.dot_general(qh, kb[:, cols], (((1,), (1,)), ((), ())),
                                preferred_element_type=F32) * scale
            s = s + bias_ref[off, h]
            m = jnp.max(s, axis=-1, keepdims=True)
            p = jnp.exp(s - m)
            p = p / jnp.sum(p, axis=-1, keepdims=True)
            o = jnp.dot(p.astype(BF16), vb[:, cols], preferred_element_type=F32)
            ssq = ssq + jnp.sum(o * o, axis=-1, keepdims=True)
            outs.append(o)
        inv = lax.rsqrt(ssq / B_W + EPS)
        for h in range(B_HEADS):
            cols = slice(h * HEAD_DIM, (h + 1) * HEAD_DIM)
            o_ref[jr * GRID_W:(jr + 1) * GRID_W, cols] = (
                outs[h] * inv * g_ref[:, cols]).astype(o_ref.dtype)


def _nb_bias_table(rel_bias):
    col = jnp.arange(GRID_W)
    c_start = jnp.clip(col - NB_COLS // 2, 0, GRID_W - NB_COLS)
    col_ok = (col[None, :] >= c_start[:, None]) & (col[None, :] < c_start[:, None] + NB_COLS)
    dc_idx = jnp.clip(col[None, :] - col[:, None] + NB_COLS - 1, 0, 2 * NB_COLS - 2)
    rb = rel_bias.astype(F32)[:, :, dc_idx]
    rb = jnp.where(col_ok[None, None], rb, NEG_INF)
    tabs = []
    for off in range(NB_ROWS):
        t = jnp.transpose(rb[:, off:off + NB_ROWS], (0, 2, 1, 3))
        tabs.append(t.reshape(B_HEADS, GRID_W, B_KEYS))
    return jnp.stack(tabs)


def _attn_b(h3, rel_bias, gain):
    bsz, s_len, _ = h3.shape
    rows = s_len // GRID_W
    assert rows >= NB_ROWS and rows % B_RQ == 0
    tq = B_RQ * GRID_W
    bias = _nb_bias_table(rel_bias)
    return pl.pallas_call(
        _attn_b_kernel,
        grid=(bsz, rows // B_RQ),
        in_specs=[pl.BlockSpec((None, tq, B_W), lambda b, i: (b, i, B_QCOL)),
                  pl.BlockSpec((None, s_len, B_W), lambda b, i: (b, 0, B_QCOL + 1)),
                  pl.BlockSpec((None, s_len, B_W), lambda b, i: (b, 0, B_QCOL + 2)),
                  pl.BlockSpec((NB_ROWS, B_HEADS, GRID_W, B_KEYS), lambda b, i: (0, 0, 0, 0)),
                  pl.BlockSpec((1, B_W), lambda b, i: (0, 0))],
        out_specs=pl.BlockSpec((None, tq, B_W), lambda b, i: (b, i, 0)),
        out_shape=jax.ShapeDtypeStruct((bsz, s_len, B_W), BF16),
        compiler_params=_cparams(("parallel", "arbitrary")),
        name="attn_b",
    )(h3, h3, h3, bias, gain.reshape(1, B_W))


C_QK = 256
C_QW = 3 * HEAD_DIM
MLA_TM = 512
MLA_TQ = 256


def _mla_proj_kernel(cq_ref, ckv_ref, kr_ref, gq_ref, gkv_ref, wq_ref, wkv_ref,
                     cos_ref, sin_ref, q_ref, k_ref, v_ref):
    cos = cos_ref[...]
    sin = sin_ref[...]
    cqn = _rms(cq_ref[...].astype(F32), gq_ref[...]).astype(BF16)
    qf = jnp.dot(cqn, wq_ref[...], preferred_element_type=F32)
    ckvn = _rms(ckv_ref[...].astype(F32), gkv_ref[...]).astype(BF16)
    kvf = jnp.dot(ckvn, wkv_ref[...], preferred_element_type=F32)
    kr = kr_ref[...].astype(F32)
    kpe = (kr[:, :HEAD_DIM] * cos + kr[:, HEAD_DIM:] * sin).astype(k_ref.dtype)
    for h in range(C_HEADS):
        b = h * C_QW
        q_ref[:, h * C_QK:h * C_QK + C_NOPE] = qf[:, b:b + C_NOPE].astype(q_ref.dtype)
        rope = qf[:, b + HEAD_DIM:b + 2 * HEAD_DIM] * cos + qf[:, b + 2 * HEAD_DIM:b + 3 * HEAD_DIM] * sin
        q_ref[:, h * C_QK + C_NOPE:(h + 1) * C_QK] = rope.astype(q_ref.dtype)
        k_ref[:, h * C_QK:h * C_QK + C_NOPE] = kvf[:, h * C_NOPE:(h + 1) * C_NOPE].astype(k_ref.dtype)
        k_ref[:, h * C_QK + C_NOPE:(h + 1) * C_QK] = kpe
    v_ref[...] = kvf[:, C_HEADS * C_NOPE:].astype(v_ref.dtype)


def _rope_tables(s_len):
    half = C_ROPE // 2
    inv = ROPE_THETA ** (-jnp.arange(half, dtype=F32) / half)
    ang = jnp.arange(s_len, dtype=F32)[:, None] * inv[None, :]
    cos = jnp.cos(ang)
    sin = jnp.sin(ang)
    zeros = jnp.zeros((s_len, HEAD_DIM - C_ROPE), F32)
    return (jnp.concatenate([cos, cos, zeros], axis=-1),
            jnp.concatenate([-sin, sin, zeros], axis=-1))


def _mla_proj(h2, s_len, gq, gkv, wq, wkv):
    n = h2.shape[0]
    tm = min(MLA_TM, s_len)
    cos, sin = _rope_tables(s_len)
    per_seq = s_len // tm
    qk_shape = jax.ShapeDtypeStruct((n, C_HEADS * C_QK), BF16)
    cq_col = (D_IN - C_ROPE - C_KV_RANK - C_Q_RANK) // C_Q_RANK
    ckv_col = (D_IN - C_ROPE - C_KV_RANK) // C_KV_RANK
    kr_col = KR_OFF // (2 * HEAD_DIM)
    return pl.pallas_call(
        _mla_proj_kernel,
        grid=(n // tm,),
        in_specs=[pl.BlockSpec((tm, C_Q_RANK), lambda i: (i, cq_col)),
                  pl.BlockSpec((tm, C_KV_RANK), lambda i: (i, ckv_col)),
                  pl.BlockSpec((tm, 2 * HEAD_DIM), lambda i: (i, kr_col)),
                  pl.BlockSpec((1, C_Q_RANK), lambda i: (0, 0)),
                  pl.BlockSpec((1, C_KV_RANK), lambda i: (0, 0)),
                  pl.BlockSpec(wq.shape, lambda i: (0, 0)),
                  pl.BlockSpec(wkv.shape, lambda i: (0, 0)),
                  pl.BlockSpec((tm, HEAD_DIM), lambda i: (i % per_seq, 0)),
                  pl.BlockSpec((tm, HEAD_DIM), lambda i: (i % per_seq, 0))],
        out_specs=[pl.BlockSpec((tm, C_HEADS * C_QK), lambda i: (i, 0)),
                   pl.BlockSpec((tm, C_HEADS * C_QK), lambda i: (i, 0)),
                   pl.BlockSpec((tm, C_OUT), lambda i: (i, 0))],
        out_shape=[qk_shape, qk_shape, jax.ShapeDtypeStruct((n, C_OUT), BF16)],
        compiler_params=_cparams(("parallel",)),
        name="mla_proj",
    )(h2, h2, h2, gq.reshape(1, -1), gkv.reshape(1, -1), wq, wkv, cos, sin)


def _mla_attn_kernel(q_ref, k_ref, v_ref, g_ref, o_ref):
    scale = (C_NOPE + C_ROPE) ** -0.5
    outs = []
    ssq = jnp.zeros((q_ref.shape[0], 1), F32)
    for h in range(C_HEADS):
        qk = slice(h * C_QK, (h + 1) * C_QK)
        s = lax.dot_general(q_ref[:, qk], k_ref[:, qk], (((1,), (1,)), ((), ())),
                            preferred_element_type=F32) * scale
        m = jnp.max(s, axis=-1, keepdims=True)
        p = jnp.exp(s - m)
        p = p / jnp.sum(p, axis=-1, keepdims=True)
        o = jnp.dot(p.astype(BF16), v_ref[:, h * C_V:(h + 1) * C_V],
                    preferred_element_type=F32)
        ssq = ssq + jnp.sum(o * o, axis=-1, keepdims=True)
        outs.append(o)
    inv = lax.rsqrt(ssq / C_OUT + EPS)
    for h in range(C_HEADS):
        cols = slice(h * C_V, (h + 1) * C_V)
        o_ref[:, cols] = (outs[h] * inv * g_ref[:, cols]).astype(o_ref.dtype)


def _mla_attn(q3, k3, v3, gain):
    bsz, s_len, _ = q3.shape
    tq = min(MLA_TQ, s_len)
    return pl.pallas_call(
        _mla_attn_kernel,
        grid=(bsz, s_len // tq),
        in_specs=[pl.BlockSpec((None, tq, C_HEADS * C_QK), lambda b, i: (b, i, 0)),
                  pl.BlockSpec((None, s_len, C_HEADS * C_QK), lambda b, i: (b, 0, 0)),
                  pl.BlockSpec((None, s_len, C_OUT), lambda b, i: (b, 0, 0)),
                  pl.BlockSpec((1, C_OUT), lambda b, i: (0, 0))],
        out_specs=pl.BlockSpec((None, tq, C_OUT), lambda b, i: (b, i, 0)),
        out_shape=jax.ShapeDtypeStruct((bsz, s_len, C_OUT), BF16),
        compiler_params=_cparams(("parallel", "arbitrary")),
        name="mla_attn",
    )(q3, k3, v3, gain.reshape(1, C_OUT))


def _out_proj_kernel(x_ref, oa_ref, ob_ref, oc_ref, wa_ref, wb_ref, wc_ref, o_ref):
    acc = jnp.dot(oa_ref[...], wa_ref[...], preferred_element_type=F32)
    acc = acc + jnp.dot(ob_ref[...], wb_ref[...], preferred_element_type=F32)
    acc = acc + jnp.dot(oc_ref[...], wc_ref[...], preferred_element_type=F32)
    o_ref[...] = x_ref[...] + acc


def _out_proj(x, oa, ob, oc, w_o, tm=1024, tn=512):
    n, d = x.shape
    tm = min(tm, n)
    return pl.pallas_call(
        _out_proj_kernel,
        grid=(n // tm, d // tn),
        in_specs=[pl.BlockSpec((tm, tn), lambda i, j: (i, j)),
                  pl.BlockSpec((tm, A_Q), lambda i, j: (i, 0)),
                  pl.BlockSpec((tm, B_W), lambda i, j: (i, 0)),
                  pl.BlockSpec((tm, C_OUT), lambda i, j: (i, 0)),
                  pl.BlockSpec((A_Q, tn), lambda i, j: (0, j)),
                  pl.BlockSpec((B_W, tn), lambda i, j: (A_Q // B_W, j)),
                  pl.BlockSpec((C_OUT, tn), lambda i, j: ((A_Q + B_W) // C_OUT, j))],
        out_specs=pl.BlockSpec((tm, tn), lambda i, j: (i, j)),
        out_shape=jax.ShapeDtypeStruct((n, d), F32),
        compiler_params=_cparams(("parallel", "arbitrary")),
        name="out_proj",
    )(x, oa, ob, oc, w_o, w_o, w_o)


ROUTE_TM = 256
_CAND_ROWS = [(a, PEER_TOPK // (a + 1)) for a in range(PEER_TOPK)]
SUBLANES = 8


def _top_ranks(s):
    idx = lax.broadcasted_iota(jnp.int32, s.shape, 0).astype(F32)
    rank = jnp.full(s.shape, float(PEER_KEYS), F32)
    vals = []
    for k in range(PEER_TOPK):
        m = jnp.max(s, axis=0, keepdims=True)
        first = jnp.min(jnp.where(s == m, idx, float(PEER_KEYS)), axis=0, keepdims=True)
        hit = idx == first
        rank = jnp.where(hit, float(k), rank)
        s = jnp.where(hit, NEG_INF, s)
        vals.append(m)
    return rank, vals


def _peer_route_kernel(x_ref, g_ref, wq_ref, keys_ref, xnt_ref, e2_ref, r2_ref, n1_ref, w1_ref):
    tm = x_ref.shape[0]
    xn = _rms(x_ref[...], g_ref[...])
    xnt = xn.T.astype(BF16)
    xnt_ref[...] = xnt
    qt = jnp.dot(wq_ref[...], xnt, preferred_element_type=F32)
    for h in range(PEER_HEADS):
        sc = []
        for c in range(2):
            row0 = (h * 2 + c) * PEER_HALF
            sc.append(jnp.dot(keys_ref[c], qt[row0:row0 + PEER_HALF, :],
                              preferred_element_type=F32,
                              precision=lax.Precision.HIGHEST))
        rank1, t1 = _top_ranks(sc[0])
        rank2, t2 = _top_ranks(sc[1])
        e1 = [jnp.exp(t1[a] - t1[0]) for a in range(PEER_TOPK)]
        e2 = [jnp.exp(t2[b] - t2[0]) for b in range(PEER_TOPK)]
        groups, gmeta = [], []
        for a, nb in _CAND_ROWS:
            for b0 in range(0, nb, SUBLANES):
                bs = list(range(b0, min(b0 + SUBLANES, nb)))
                rows = [t1[a] + t2[b] for b in bs]
                rows += [jnp.full((1, tm), NEG_INF, F32)] * (SUBLANES - len(bs))
                groups.append(jnp.concatenate(rows, axis=0))
                gmeta.append((a, bs))
        cand = jnp.concatenate(groups, axis=0)
        cidx = lax.broadcasted_iota(jnp.int32, cand.shape, 0).astype(F32)
        self_f = jnp.zeros(cand.shape, F32)
        for _ in range(PEER_TOPK):
            m = jnp.max(cand, axis=0, keepdims=True)
            first = jnp.min(jnp.where(cand == m, cidx, float(cand.shape[0])), axis=0, keepdims=True)
            hit = cidx == first
            self_f = jnp.where(hit, 1.0, self_f)
            cand = jnp.where(hit, NEG_INF, cand)
        n_a = [jnp.zeros((1, tm), F32) for _ in range(PEER_TOPK)]
        z = jnp.zeros((1, tm), F32)
        for gi, (a, bs) in enumerate(gmeta):
            blk = self_f[gi * SUBLANES:(gi + 1) * SUBLANES, :]
            n_a[a] = n_a[a] + jnp.sum(blk, axis=0, keepdims=True)
            for bi, b in enumerate(bs):
                z = z + blk[bi:bi + 1, :] * (e1[a] * e2[b])
        n1 = jnp.zeros(rank1.shape, F32)
        for a in range(PEER_TOPK):
            n1 = jnp.where(rank1 == float(a), n_a[a], n1)
        e2_ref[h] = jnp.exp(sc[1] - t2[0])
        r2_ref[h] = rank2
        n1_ref[h] = n1.reshape(PEER_KEYS // SUBLANES, SUBLANES, tm)
        w1_ref[h] = (jnp.exp(sc[0] - t1[0]) / z).reshape(PEER_KEYS // SUBLANES, SUBLANES, tm)


def _peer_route(x, g, wq_t, sub_keys):
    n, d = x.shape
    tm = min(ROUTE_TM, n)
    plane = jax.ShapeDtypeStruct((PEER_HEADS, PEER_KEYS, n), F32)
    plane_spec = pl.BlockSpec((PEER_HEADS, PEER_KEYS, tm), lambda i: (0, 0, i))
    table = jax.ShapeDtypeStruct((PEER_HEADS, PEER_KEYS // SUBLANES, SUBLANES, n), F32)
    table_spec = pl.BlockSpec((PEER_HEADS, PEER_KEYS // SUBLANES, SUBLANES, tm),
                              lambda i: (0, 0, 0, i))
    return pl.pallas_call(
        _peer_route_kernel,
        grid=(n // tm,),
        in_specs=[pl.BlockSpec((tm, d), lambda i: (i, 0)),
                  pl.BlockSpec((1, d), lambda i: (0, 0)),
                  pl.BlockSpec(wq_t.shape, lambda i: (0, 0)),
                  pl.BlockSpec(sub_keys.shape, lambda i: (0, 0, 0))],
        out_specs=[pl.BlockSpec((d, tm), lambda i: (0, i)),
                   plane_spec, plane_spec, table_spec, table_spec],
        out_shape=[jax.ShapeDtypeStruct((d, n), BF16), plane, plane, table, table],
        compiler_params=_cparams(("parallel",)),
        name="peer_route",
    )(x, g.reshape(1, d), wq_t, sub_keys)


PEER_T = 512
PEER_ET = SUBLANES * PEER_KEYS
LANES = 128


def _peer_dense_kernel(u_ref, vt_ref, xnt_ref, e2_ref, r2_ref, n1_ref, w1_ref, yt_ref, h_ref):
    e = pl.program_id(1)
    t = xnt_ref.shape[1]

    @pl.when(e == 0)
    def _():
        yt_ref[...] = jnp.zeros_like(yt_ref)

    at = jnp.dot(u_ref[...], xnt_ref[...], preferred_element_type=F32)
    for rl in range(SUBLANES):
        for tc in range(t // LANES):
            cols = slice(tc * LANES, (tc + 1) * LANES)
            gate = jnp.zeros((PEER_KEYS, LANES), F32)
            for h in range(PEER_HEADS):
                n1r = n1_ref[h, e, rl:rl + 1, cols]
                w1r = w1_ref[h, e, rl:rl + 1, cols]
                gate = gate + jnp.where(r2_ref[h, :, cols] < n1r, e2_ref[h, :, cols] * w1r, 0.0)
            a = at[rl * PEER_KEYS:(rl + 1) * PEER_KEYS, cols]
            h_ref[rl * PEER_KEYS:(rl + 1) * PEER_KEYS, cols] = (jax.nn.gelu(a) * gate).astype(BF16)
    yt_ref[...] += jnp.dot(vt_ref[...], h_ref[...], preferred_element_type=F32)


def _peer_dense(u, vt, xnt, e2, r2, n1, w1):
    n_exp, d = u.shape
    n = xnt.shape[1]
    t = min(PEER_T, n)
    plane_spec = pl.BlockSpec((PEER_HEADS, PEER_KEYS, t), lambda i, e: (0, 0, i))
    table_spec = pl.BlockSpec((PEER_HEADS, PEER_KEYS // SUBLANES, SUBLANES, t),
                              lambda i, e: (0, 0, 0, i))
    return pl.pallas_call(
        _peer_dense_kernel,
        grid=(n // t, n_exp // PEER_ET),
        in_specs=[pl.BlockSpec((PEER_ET, d), lambda i, e: (e, 0)),
                  pl.BlockSpec((d, PEER_ET), lambda i, e: (0, e)),
                  pl.BlockSpec((d, t), lambda i, e: (0, i)),
                  plane_spec, plane_spec, table_spec, table_spec],
        out_specs=pl.BlockSpec((d, t), lambda i, e: (0, i)),
        out_shape=jax.ShapeDtypeStruct((d, n), F32),
        scratch_shapes=[pltpu.VMEM((PEER_ET, t), BF16)],
        compiler_params=_cparams(("parallel", "arbitrary")),
        name="peer_dense",
    )(u, vt, xnt, e2, r2, n1, w1)


def _add_t_kernel(x_ref, yt_ref, o_ref):
    o_ref[...] = x_ref[...] + yt_ref[...].T


def _add_t_norm_kernel(x_ref, yt_ref, g_ref, o_ref):
    o_ref[...] = _rms(x_ref[...] + yt_ref[...].T, g_ref[...])


def _add_transposed(x, yt, gain=None, tm=512):
    n, d = x.shape
    tm = min(tm, n)
    in_specs = [pl.BlockSpec((tm, d), lambda i: (i, 0)),
                pl.BlockSpec((d, tm), lambda i: (0, i))]
    args = [x, yt]
    body = _add_t_kernel
    if gain is not None:
        in_specs.append(pl.BlockSpec((1, d), lambda i: (0, 0)))
        args.append(gain.reshape(1, d))
        body = _add_t_norm_kernel
    return pl.pallas_call(
        body,
        grid=(n // tm,),
        in_specs=in_specs,
        out_specs=pl.BlockSpec((tm, d), lambda i: (i, 0)),
        out_shape=jax.ShapeDtypeStruct((n, d), F32),
        compiler_params=_cparams(("parallel",)),
        name="add_transposed",
    )(*args)


def _swap_halves(w):
    half = w.shape[-1] // 2
    return jnp.concatenate([w[..., half:], w[..., :half]], axis=-1)


def _prep_w_in(w_in):
    d = w_in.shape[0]
    kr = w_in[:, KR_OFF:D_IN]
    z = jnp.zeros((d, HEAD_DIM - C_ROPE), w_in.dtype)
    return jnp.concatenate([w_in[:, :KR_OFF], kr, z, _swap_halves(kr), z], axis=-1).astype(BF16)


def _prep_w_uq(w_uq):
    r = w_uq.shape[0]
    z = jnp.zeros((r, HEAD_DIM - C_ROPE), w_uq.dtype)
    blocks = []
    for h in range(C_HEADS):
        b = h * (C_NOPE + C_ROPE)
        rope = w_uq[:, b + C_NOPE:b + C_NOPE + C_ROPE]
        blocks += [w_uq[:, b:b + C_NOPE], rope, z, _swap_halves(rope), z]
    return jnp.concatenate(blocks, axis=-1).astype(BF16)


def _prep_w_ukv(w_ukv):
    ks = [w_ukv[:, h * (C_NOPE + C_V):h * (C_NOPE + C_V) + C_NOPE] for h in range(C_HEADS)]
    vs = [w_ukv[:, h * (C_NOPE + C_V) + C_NOPE:(h + 1) * (C_NOPE + C_V)] for h in range(C_HEADS)]
    return jnp.concatenate(ks + vs, axis=-1).astype(BF16)


def _mixer(x2, bsz, s_len, ln1, w_in, a_sink, b_rel_bias, c_q_norm, c_kv_norm, c_w_uq, c_w_ukv,
           out_norm, w_o):
    h2 = _norm_matmul(x2, ln1, _prep_w_in(w_in))
    h3 = h2.reshape(bsz, s_len, D_IN_PAD)
    oa = _attn_a(h3, a_sink.astype(F32), out_norm[:A_Q])
    ob = _attn_b(h3, b_rel_bias, out_norm[A_Q:A_Q + B_W])
    qc, kc, vc = _mla_proj(h2, s_len, c_q_norm, c_kv_norm, _prep_w_uq(c_w_uq), _prep_w_ukv(c_w_ukv))
    oc = _mla_attn(qc.reshape(bsz, s_len, -1), kc.reshape(bsz, s_len, -1),
                   vc.reshape(bsz, s_len, -1), out_norm[A_Q + B_W:])
    n = bsz * s_len
    return _out_proj(x2, oa.reshape(n, A_Q), ob.reshape(n, B_W), oc.reshape(n, C_OUT),
                     w_o.astype(BF16))


def _peer(x2, ln2, w_q, sub_keys, u, vv):
    xnt, e2, r2, n1, w1 = _peer_route(x2, ln2, w_q.T.astype(BF16), sub_keys.astype(F32))
    return _peer_dense(u.astype(BF16), vv.T.astype(BF16), xnt, e2, r2, n1, w1)


def kernel(x, ln1, w_in, a_sink, b_rel_bias, c_q_norm, c_kv_norm, c_w_uq, c_w_ukv, out_norm, w_o,
           ln2, peer_w_q, peer_sub_keys, peer_u, peer_v, final_norm):
    bsz, s_len, d = x.shape
    depth = ln1.shape[0]
    x2 = x.reshape(bsz * s_len, d)
    for l in range(depth):
        x2 = _mixer(x2, bsz, s_len, ln1[l], w_in[l], a_sink[l], b_rel_bias[l], c_q_norm[l],
                    c_kv_norm[l], c_w_uq[l], c_w_ukv[l], out_norm[l], w_o[l])
        yt = _peer(x2, ln2[l], peer_w_q[l], peer_sub_keys[l], peer_u[l], peer_v[l])
        x2 = _add_transposed(x2, yt, final_norm if l == depth - 1 else None)
    return x2.reshape(bsz, s_len, d)
```

```python
import functools
import math

import jax
import jax.numpy as jnp
import numpy as np
from jax import lax
from jax.experimental import pallas as pl
from jax.experimental.pallas import tpu as pltpu

F32 = jnp.float32
BF16 = jnp.bfloat16

EPS = 1e-6
HEAD_DIM = 128
A_HEADS = 8
A_KV_HEADS = 2
A_GROUP = A_HEADS // A_KV_HEADS
A_WINDOW = 128
B_HEADS = 4
GRID_W = 64
NB_ROWS = 8
NB_COLS = 16
C_HEADS = 4
C_Q_RANK = 512
C_KV_RANK = 256
C_NOPE = 128
C_ROPE = 64
C_V = 128
ROPE_THETA = 10000.0
A_Q = A_HEADS * HEAD_DIM
A_KV = A_KV_HEADS * HEAD_DIM
B_W = B_HEADS * HEAD_DIM
C_OUT = C_HEADS * C_V
D_MIX = A_Q + B_W + C_OUT
D_IN = A_Q + 2 * A_KV + 3 * B_W + C_Q_RANK + C_KV_RANK + C_ROPE
D_IN_PAD = 4096
KR_OFF = D_IN - C_ROPE
PEER_HEADS = 8
PEER_KEYS = 128
PEER_HALF = 64
PEER_TOPK = 16

VMEM_LIMIT = 56 * 1024 * 1024
NEG_INF = float("-inf")


def _cparams(sem):
    return pltpu.CompilerParams(dimension_semantics=sem, vmem_limit_bytes=VMEM_LIMIT)


def _rms(x, g):
    ms = jnp.mean(x * x, axis=-1, keepdims=True)
    return x * lax.rsqrt(ms + EPS) * g


def _norm_matmul_kernel(x_ref, g_ref, w_ref, o_ref, xn_ref):
    @pl.when(pl.program_id(1) == 0)
    def _():
        xn_ref[...] = _rms(x_ref[...], g_ref[...]).astype(BF16)

    o_ref[...] = jnp.dot(xn_ref[...], w_ref[...],
                         preferred_element_type=F32).astype(o_ref.dtype)


def _norm_matmul(x, g, w, tm=1024, tn=512):
    n, d = x.shape
    nout = w.shape[1]
    tm = min(tm, n)
    return pl.pallas_call(
        _norm_matmul_kernel,
        grid=(n // tm, nout // tn),
        in_specs=[pl.BlockSpec((tm, d), lambda i, j: (i, 0)),
                  pl.BlockSpec((1, d), lambda i, j: (0, 0)),
                  pl.BlockSpec((d, tn), lambda i, j: (0, j))],
        out_specs=pl.BlockSpec((tm, tn), lambda i, j: (i, j)),
        out_shape=jax.ShapeDtypeStruct((n, nout), BF16),
        scratch_shapes=[pltpu.VMEM((tm, d), BF16)],
        compiler_params=_cparams(("parallel", "arbitrary")),
        name="norm_matmul",
    )(x, g.reshape(1, d), w)


A_TQ = 512
A_BAND = 3 * A_WINDOW


def _attn_a_kernel(sink_ref, q_ref, k_ref, v_ref, g_ref, o_ref):
    i = pl.program_id(1)
    s_len = k_ref.shape[0]
    scale = HEAD_DIM ** -0.5
    slopes = [2.0 ** (-8.0 * (h + 1) / A_HEADS) for h in range(A_HEADS)]
    for j in range(A_TQ // A_WINDOW):
        q0 = i * A_TQ + j * A_WINDOW
        start = pl.multiple_of(jnp.clip(q0 - A_WINDOW, 0, s_len - A_BAND), A_WINDOW)
        kb = k_ref[pl.ds(start, A_BAND), :]
        vb = v_ref[pl.ds(start, A_BAND), :]
        q_pos = q0 + lax.broadcasted_iota(jnp.int32, (A_WINDOW, A_BAND), 0)
        k_pos = start + lax.broadcasted_iota(jnp.int32, (A_WINDOW, A_BAND), 1)
        dist = jnp.abs(q_pos - k_pos)
        valid = dist <= A_WINDOW
        dist_f = dist.astype(F32)
        outs = []
        ssq = jnp.zeros((A_WINDOW, 1), F32)
        for kh in range(A_KV_HEADS):
            kk = kb[:, kh * HEAD_DIM:(kh + 1) * HEAD_DIM]
            vv = vb[:, kh * HEAD_DIM:(kh + 1) * HEAD_DIM]
            for g in range(A_GROUP):
                h = kh * A_GROUP + g
                qh = q_ref[j * A_WINDOW:(j + 1) * A_WINDOW, h * HEAD_DIM:(h + 1) * HEAD_DIM]
                s = lax.dot_general(qh, kk, (((1,), (1,)), ((), ())),
                                    preferred_element_type=F32) * scale
                s = s - slopes[h] * dist_f
                s = jnp.where(valid, s, NEG_INF)
                sink = sink_ref[h]
                m = jnp.maximum(jnp.max(s, axis=-1, keepdims=True), sink)
                p = jnp.exp(s - m)
                denom = jnp.sum(p, axis=-1, keepdims=True) + jnp.exp(sink - m)
                p = p / denom
                o = jnp.dot(p.astype(BF16), vv, preferred_element_type=F32)
                ssq = ssq + jnp.sum(o * o, axis=-1, keepdims=True)
                outs.append(o)
        inv = lax.rsqrt(ssq / A_Q + EPS)
        for h in range(A_HEADS):
            cols = slice(h * HEAD_DIM, (h + 1) * HEAD_DIM)
            o_ref[j * A_WINDOW:(j + 1) * A_WINDOW, cols] = (
                outs[h] * inv * g_ref[:, cols]).astype(o_ref.dtype)


def _attn_a(h3, sink, gain):
    bsz, s_len, _ = h3.shape
    assert s_len % A_TQ == 0 and s_len >= A_BAND
    kcol = A_Q // A_KV
    return pl.pallas_call(
        _attn_a_kernel,
        grid=(bsz, s_len // A_TQ),
        in_specs=[pl.BlockSpec(memory_space=pltpu.SMEM),
                  pl.BlockSpec((None, A_TQ, A_Q), lambda b, i: (b, i, 0)),
                  pl.BlockSpec((None, s_len, A_KV), lambda b, i: (b, 0, kcol)),
                  pl.BlockSpec((None, s_len, A_KV), lambda b, i: (b, 0, kcol + 1)),
                  pl.BlockSpec((1, A_Q), lambda b, i: (0, 0))],
        out_specs=pl.BlockSpec((None, A_TQ, A_Q), lambda b, i: (b, i, 0)),
        out_shape=jax.ShapeDtypeStruct((bsz, s_len, A_Q), BF16),
        compiler_params=_cparams(("parallel", "arbitrary")),
        name="attn_a",
    )(sink, h3, h3, h3, gain.reshape(1, A_Q))


B_RQ = 8
B_KEYS = NB_ROWS * GRID_W
B_QCOL = (A_Q + 2 * A_KV) // B_W


def _attn_b_kernel(q_ref, k_ref, v_ref, bias_ref, g_ref, o_ref):
    i = pl.program_id(1)
    rows = k_ref.shape[0] // GRID_W
    scale = HEAD_DIM ** -0.5
    for jr in range(B_RQ):
        r = i * B_RQ + jr
        r_start = jnp.clip(r - NB_ROWS // 2, 0, rows - NB_ROWS)
        off = r_start - r + NB_ROWS - 1
        kstart = pl.multiple_of(r_start * GRID_W, GRID_W)
        kb = k_ref[pl.ds(kstart, B_KEYS), :]
        vb = v_ref[pl.ds(kstart, B_KEYS), :]
        outs = []
        ssq = jnp.zeros((GRID_W, 1), F32)
        for h in range(B_HEADS):
            cols = slice(h * HEAD_DIM, (h + 1) * HEAD_DIM)
            qh = q_ref[jr * GRID_W:(jr + 1) * GRID_W, cols]
            s = lax.dot_general(qh, kb[:, cols], (((1,), (1,)), ((), ())),
                                preferred_element_type=F32) * scale
            s = s + bias_ref[off, h]
            m = jnp.max(s, axis=-1, keepdims=True)
            p = jnp.exp(s - m)
            p = p / jnp.sum(p, axis=-1, keepdims=True)
            o = jnp.dot(p.astype(BF16), vb[:, cols], preferred_element_type=F32)
            ssq = ssq + jnp.sum(o * o, axis=-1, keepdims=True)
            outs.append(o)
        inv = lax.rsqrt(ssq / B_W + EPS)
        for h in range(B_HEADS):
            cols = slice(h * HEAD_DIM, (h + 1) * HEAD_DIM)
            o_ref[jr * GRID_W:(jr + 1) * GRID_W, cols] = (
                outs[h] * inv * g_ref[:, cols]).astype(o_ref.dtype)


def _nb_bias_table(rel_bias):
    col = jnp.arange(GRID_W)
    c_start = jnp.clip(col - NB_COLS // 2, 0, GRID_W - NB_COLS)
    col_ok = (col[None, :] >= c_start[:, None]) & (col[None, :] < c_start[:, None] + NB_COLS)
    dc_idx = jnp.clip(col[None, :] - col[:, None] + NB_COLS - 1, 0, 2 * NB_COLS - 2)
    rb = rel_bias.astype(F32)[:, :, dc_idx]
    rb = jnp.where(col_ok[None, None], rb, NEG_INF)
    tabs = []
    for off in range(NB_ROWS):
        t = jnp.transpose(rb[:, off:off + NB_ROWS], (0, 2, 1, 3))
        tabs.append(t.reshape(B_HEADS, GRID_W, B_KEYS))
    return jnp.stack(tabs)


def _attn_b(h3, rel_bias, gain):
    bsz, s_len, _ = h3.shape
    rows = s_len // GRID_W
    assert rows >= NB_ROWS and rows % B_RQ == 0
    tq = B_RQ * GRID_W
    bias = _nb_bias_table(rel_bias)
    return pl.pallas_call(
        _attn_b_kernel,
        grid=(bsz, rows // B_RQ),
        in_specs=[pl.BlockSpec((None, tq, B_W), lambda b, i: (b, i, B_QCOL)),
                  pl.BlockSpec((None, s_len, B_W), lambda b, i: (b, 0, B_QCOL + 1)),
                  pl.BlockSpec((None, s_len, B_W), lambda b, i: (b, 0, B_QCOL + 2)),
                  pl.BlockSpec((NB_ROWS, B_HEADS, GRID_W, B_KEYS), lambda b, i: (0, 0, 0, 0)),
                  pl.BlockSpec((1, B_W), lambda b, i: (0, 0))],
        out_specs=pl.BlockSpec((None, tq, B_W), lambda b, i: (b, i, 0)),
        out_shape=jax.ShapeDtypeStruct((bsz, s_len, B_W), BF16),
        compiler_params=_cparams(("parallel", "arbitrary")),
        name="attn_b",
    )(h3, h3, h3, bias, gain.reshape(1, B_W))


C_QK = 256
C_QW = 3 * HEAD_DIM
MLA_TM = 512
MLA_TQ = 256


def _mla_proj_kernel(cq_ref, ckv_ref, kr_ref, gq_ref, gkv_ref, wq_ref, wkv_ref,
                     cos_ref, sin_ref, q_ref, k_ref, v_ref):
    cos = cos_ref[...]
    sin = sin_ref[...]
    cqn = _rms(cq_ref[...].astype(F32), gq_ref[...]).astype(BF16)
    qf = jnp.dot(cqn, wq_ref[...], preferred_element_type=F32)
    ckvn = _rms(ckv_ref[...].astype(F32), gkv_ref[...]).astype(BF16)
    kvf = jnp.dot(ckvn, wkv_ref[...], preferred_element_type=F32)
    kr = kr_ref[...].astype(F32)
    kpe = (kr[:, :HEAD_DIM] * cos + kr[:, HEAD_DIM:] * sin).astype(k_ref.dtype)
    for h in range(C_HEADS):
        b = h * C_QW
        q_ref[:, h * C_QK:h * C_QK + C_NOPE] = qf[:, b:b + C_NOPE].astype(q_ref.dtype)
        rope = qf[:, b + HEAD_DIM:b + 2 * HEAD_DIM] * cos + qf[:, b + 2 * HEAD_DIM:b + 3 * HEAD_DIM] * sin
        q_ref[:, h * C_QK + C_NOPE:(h + 1) * C_QK] = rope.astype(q_ref.dtype)
        k_ref[:, h * C_QK:h * C_QK + C_NOPE] = kvf[:, h * C_NOPE:(h + 1) * C_NOPE].astype(k_ref.dtype)
        k_ref[:, h * C_QK + C_NOPE:(h + 1) * C_QK] = kpe
    v_ref[...] = kvf[:, C_HEADS * C_NOPE:].astype(v_ref.dtype)


def _rope_tables(s_len):
    half = C_ROPE // 2
    inv = ROPE_THETA ** (-jnp.arange(half, dtype=F32) / half)
    ang = jnp.arange(s_len, dtype=F32)[:, None] * inv[None, :]
    cos = jnp.cos(ang)
    sin = jnp.sin(ang)
    zeros = jnp.zeros((s_len, HEAD_DIM - C_ROPE), F32)
    return (jnp.concatenate([cos, cos, zeros], axis=-1),
            jnp.concatenate([-sin, sin, zeros], axis=-1))


def _mla_proj(h2, s_len, gq, gkv, wq, wkv):
    n = h2.shape[0]
    tm = min(MLA_TM, s_len)
    cos, sin = _rope_tables(s_len)
    per_seq = s_len // tm
    qk_shape = jax.ShapeDtypeStruct((n, C_HEADS * C_QK), BF16)
    cq_col = (D_IN - C_ROPE - C_KV_RANK - C_Q_RANK) // C_Q_RANK
    ckv_col = (D_IN - C_ROPE - C_KV_RANK) // C_KV_RANK
    kr_col = KR_OFF // (2 * HEAD_DIM)
    return pl.pallas_call(
        _mla_proj_kernel,
        grid=(n // tm,),
        in_specs=[pl.BlockSpec((tm, C_Q_RANK), lambda i: (i, cq_col)),
                  pl.BlockSpec((tm, C_KV_RANK), lambda i: (i, ckv_col)),
                  pl.BlockSpec((tm, 2 * HEAD_DIM), lambda i: (i, kr_col)),
                  pl.BlockSpec((1, C_Q_RANK), lambda i: (0, 0)),
                  pl.BlockSpec((1, C_KV_RANK), lambda i: (0, 0)),
                  pl.BlockSpec(wq.shape, lambda i: (0, 0)),
                  pl.BlockSpec(wkv.shape, lambda i: (0, 0)),
                  pl.BlockSpec((tm, HEAD_DIM), lambda i: (i % per_seq, 0)),
                  pl.BlockSpec((tm, HEAD_DIM), lambda i: (i % per_seq, 0))],
        out_specs=[pl.BlockSpec((tm, C_HEADS * C_QK), lambda i: (i, 0)),
                   pl.BlockSpec((tm, C_HEADS * C_QK), lambda i: (i, 0)),
                   pl.BlockSpec((tm, C_OUT), lambda i: (i, 0))],
        out_shape=[qk_shape, qk_shape, jax.ShapeDtypeStruct((n, C_OUT), BF16)],
        compiler_params=_cparams(("parallel",)),
        name="mla_proj",
    )(h2, h2, h2, gq.reshape(1, -1), gkv.reshape(1, -1), wq, wkv, cos, sin)


def _mla_attn_kernel(q_ref, k_ref, v_ref, g_ref, o_ref):
    scale = (C_NOPE + C_ROPE) ** -0.5
    outs = []
    ssq = jnp.zeros((q_ref.shape[0], 1), F32)
    for h in range(C_HEADS):
        qk = slice(h * C_QK, (h + 1) * C_QK)
        s = lax.dot_general(q_ref[:, qk], k_ref[:, qk], (((1,), (1,)), ((), ())),
                            preferred_element_type=F32) * scale
        m = jnp.max(s, axis=-1, keepdims=True)
        p = jnp.exp(s - m)
        p = p / jnp.sum(p, axis=-1, keepdims=True)
        o = jnp.dot(p.astype(BF16), v_ref[:, h * C_V:(h + 1) * C_V],
                    preferred_element_type=F32)
        ssq = ssq + jnp.sum(o * o, axis=-1, keepdims=True)
        outs.append(o)
    inv = lax.rsqrt(ssq / C_OUT + EPS)
    for h in range(C_HEADS):
        cols = slice(h * C_V, (h + 1) * C_V)
        o_ref[:, cols] = (outs[h] * inv * g_ref[:, cols]).astype(o_ref.dtype)


def _mla_attn(q3, k3, v3, gain):
    bsz, s_len, _ = q3.shape
    tq = min(MLA_TQ, s_len)
    return pl.pallas_call(
        _mla_attn_kernel,
        grid=(bsz, s_len // tq),
        in_specs=[pl.BlockSpec((None, tq, C_HEADS * C_QK), lambda b, i: (b, i, 0)),
                  pl.BlockSpec((None, s_len, C_HEADS * C_QK), lambda b, i: (b, 0, 0)),
                  pl.BlockSpec((None, s_len, C_OUT), lambda b, i: (b, 0, 0)),
                  pl.BlockSpec((1, C_OUT), lambda b, i: (0, 0))],
        out_specs=pl.BlockSpec((None, tq, C_OUT), lambda b, i: (b, i, 0)),
        out_shape=jax.ShapeDtypeStruct((bsz, s_len, C_OUT), BF16),
        compiler_params=_cparams(("parallel", "arbitrary")),
        name="mla_attn",
    )(q3, k3, v3, gain.reshape(1, C_OUT))


def _out_proj_kernel(x_ref, oa_ref, ob_ref, oc_ref, wa_ref, wb_ref, wc_ref, o_ref):
    acc = jnp.dot(oa_ref[...], wa_ref[...], preferred_element_type=F32)
    acc = acc + jnp.dot(ob_ref[...], wb_ref[...], preferred_element_type=F32)
    acc = acc + jnp.dot(oc_ref[...], wc_ref[...], preferred_element_type=F32)
    o_ref[...] = x_ref[...] + acc


def _out_proj(x, oa, ob, oc, w_o, tm=1024, tn=512):
    n, d = x.shape
    tm = min(tm, n)
    return pl.pallas_call(
        _out_proj_kernel,
        grid=(n // tm, d // tn),
        in_specs=[pl.BlockSpec((tm, tn), lambda i, j: (i, j)),
                  pl.BlockSpec((tm, A_Q), lambda i, j: (i, 0)),
                  pl.BlockSpec((tm, B_W), lambda i, j: (i, 0)),
                  pl.BlockSpec((tm, C_OUT), lambda i, j: (i, 0)),
                  pl.BlockSpec((A_Q, tn), lambda i, j: (0, j)),
                  pl.BlockSpec((B_W, tn), lambda i, j: (A_Q // B_W, j)),
                  pl.BlockSpec((C_OUT, tn), lambda i, j: ((A_Q + B_W) // C_OUT, j))],
        out_specs=pl.BlockSpec((tm, tn), lambda i, j: (i, j)),
        out_shape=jax.ShapeDtypeStruct((n, d), F32),
        compiler_params=_cparams(("parallel", "arbitrary")),
        name="out_proj",
    )(x, oa, ob, oc, w_o, w_o, w_o)


ROUTE_TM = 256
_CAND_ROWS = [(a, PEER_TOPK // (a + 1)) for a in range(PEER_TOPK)]
SUBLANES = 8


def _top_ranks(s):
    idx = lax.broadcasted_iota(jnp.int32, s.shape, 0).astype(F32)
    rank = jnp.full(s.shape, float(PEER_KEYS), F32)
    vals = []
    for k in range(PEER_TOPK):
        m = jnp.max(s, axis=0, keepdims=True)
        first = jnp.min(jnp.where(s == m, idx, float(PEER_KEYS)), axis=0, keepdims=True)
        hit = idx == first
        rank = jnp.where(hit, float(k), rank)
        s = jnp.where(hit, NEG_INF, s)
        vals.append(m)
    return rank, vals


def _peer_route_kernel(x_ref, g_ref, wq_ref, keys_ref, xnt_ref, e2_ref, r2_ref, n1_ref, w1_ref):
    tm = x_ref.shape[0]
    xn = _rms(x_ref[...], g_ref[...])
    xnt = xn.T.astype(BF16)
    xnt_ref[...] = xnt
    qt = jnp.dot(wq_ref[...], xnt, preferred_element_type=F32)
    for h in range(PEER_HEADS):
        sc = []
        for c in range(2):
            row0 = (h * 2 + c) * PEER_HALF
            sc.append(jnp.dot(keys_ref[c], qt[row0:row0 + PEER_HALF, :],
                              preferred_element_type=F32,
                              precision=lax.Precision.HIGHEST))
        rank1, t1 = _top_ranks(sc[0])
        rank2, t2 = _top_ranks(sc[1])
        e1 = [jnp.exp(t1[a] - t1[0]) for a in range(PEER_TOPK)]
        e2 = [jnp.exp(t2[b] - t2[0]) for b in range(PEER_TOPK)]
        groups, gmeta = [], []
        for a, nb in _CAND_ROWS:
            for b0 in range(0, nb, SUBLANES):
                bs = list(range(b0, min(b0 + SUBLANES, nb)))
                rows = [t1[a] + t2[b] for b in bs]
                rows += [jnp.full((1, tm), NEG_INF, F32)] * (SUBLANES - len(bs))
                groups.append(jnp.concatenate(rows, axis=0))
                gmeta.append((a, bs))
        cand = jnp.concatenate(groups, axis=0)
        cidx = lax.broadcasted_iota(jnp.int32, cand.shape, 0).astype(F32)
        self_f = jnp.zeros(cand.shape, F32)
        for _ in range(PEER_TOPK):
            m = jnp.max(cand, axis=0, keepdims=True)
            first = jnp.min(jnp.where(cand == m, cidx, float(cand.shape[0])), axis=0, keepdims=True)
            hit = cidx == first
            self_f = jnp.where(hit, 1.0, self_f)
            cand = jnp.where(hit, NEG_INF, cand)
        n_a = [jnp.zeros((1, tm), F32) for _ in range(PEER_TOPK)]
        z = jnp.zeros((1, tm), F32)
        for gi, (a, bs) in enumerate(gmeta):
            blk = self_f[gi * SUBLANES:(gi + 1) * SUBLANES, :]
            n_a[a] = n_a[a] + jnp.sum(blk, axis=0, keepdims=True)
            for bi, b in enumerate(bs):
                z = z + blk[bi:bi + 1, :] * (e1[a] * e2[b])
        n1 = jnp.zeros(rank1.shape, F32)
        for a in range(PEER_TOPK):
            n1 = jnp.where(rank1 == float(a), n_a[a], n1)
        e2_ref[h] = jnp.exp(sc[1] - t2[0]).astype(e2_ref.dtype)
        r2_ref[h] = rank2.astype(r2_ref.dtype)
        n1_ref[h] = n1.reshape(PEER_KEYS // SUBLANES, SUBLANES, tm)
        w1_ref[h] = (jnp.exp(sc[0] - t1[0]) / z).reshape(PEER_KEYS // SUBLANES, SUBLANES, tm)


def _peer_route(x, g, wq_t, sub_keys):
    n, d = x.shape
    tm = min(ROUTE_TM, n)
    plane = jax.ShapeDtypeStruct((PEER_HEADS, PEER_KEYS, n), BF16)
    plane_spec = pl.BlockSpec((PEER_HEADS, PEER_KEYS, tm), lambda i: (0, 0, i))
    table = jax.ShapeDtypeStruct((PEER_HEADS, PEER_KEYS // SUBLANES, SUBLANES, n), F32)
    table_spec = pl.BlockSpec((PEER_HEADS, PEER_KEYS // SUBLANES, SUBLANES, tm),
                              lambda i: (0, 0, 0, i))
    return pl.pallas_call(
        _peer_route_kernel,
        grid=(n // tm,),
        in_specs=[pl.BlockSpec((tm, d), lambda i: (i, 0)),
                  pl.BlockSpec((1, d), lambda i: (0, 0)),
                  pl.BlockSpec(wq_t.shape, lambda i: (0, 0)),
                  pl.BlockSpec(sub_keys.shape, lambda i: (0, 0, 0))],
        out_specs=[pl.BlockSpec((d, tm), lambda i: (0, i)),
                   plane_spec, plane_spec, table_spec, table_spec],
        out_shape=[jax.ShapeDtypeStruct((d, n), BF16), plane, plane, table, table],
        compiler_params=_cparams(("parallel",)),
        name="peer_route",
    )(x, g.reshape(1, d), wq_t, sub_keys)


PEER_T = 512
PEER_ET = SUBLANES * PEER_KEYS
LANES = 128


def _peer_dense_kernel(u_ref, vt_ref, xnt_ref, e2_ref, r2_ref, n1_ref, w1_ref, yt_ref, h_ref):
    e = pl.program_id(1)
    t = xnt_ref.shape[1]

    @pl.when(e == 0)
    def _():
        yt_ref[...] = jnp.zeros_like(yt_ref)

    at = jnp.dot(u_ref[...], xnt_ref[...], preferred_element_type=F32)
    for rl in range(SUBLANES):
        for tc in range(t // LANES):
            cols = slice(tc * LANES, (tc + 1) * LANES)
            gate = jnp.zeros((PEER_KEYS, LANES), BF16)
            for h in range(PEER_HEADS):
                n1r = jnp.broadcast_to(n1_ref[h, e, rl:rl + 1, cols], (PEER_KEYS, LANES)).astype(BF16)
                w1r = jnp.broadcast_to(w1_ref[h, e, rl:rl + 1, cols], (PEER_KEYS, LANES)).astype(BF16)
                gate = gate + jnp.where(r2_ref[h, :, cols] < n1r, e2_ref[h, :, cols] * w1r,
                                        jnp.zeros((), BF16))
            a = at[rl * PEER_KEYS:(rl + 1) * PEER_KEYS, cols]
            h_ref[rl * PEER_KEYS:(rl + 1) * PEER_KEYS, cols] = jax.nn.gelu(a).astype(BF16) * gate
    yt_ref[...] += jnp.dot(vt_ref[...], h_ref[...], preferred_element_type=F32)


def _peer_dense(u, vt, xnt, e2, r2, n1, w1):
    n_exp, d = u.shape
    n = xnt.shape[1]
    t = min(PEER_T, n)
    plane_spec = pl.BlockSpec((PEER_HEADS, PEER_KEYS, t), lambda i, e: (0, 0, i))
    table_spec = pl.BlockSpec((PEER_HEADS, PEER_KEYS // SUBLANES, SUBLANES, t),
                              lambda i, e: (0, 0, 0, i))
    return pl.pallas_call(
        _peer_dense_kernel,
        grid=(n // t, n_exp // PEER_ET),
        in_specs=[pl.BlockSpec((PEER_ET, d), lambda i, e: (e, 0)),
                  pl.BlockSpec((d, PEER_ET), lambda i, e: (0, e)),
                  pl.BlockSpec((d, t), lambda i, e: (0, i)),
                  plane_spec, plane_spec, table_spec, table_spec],
        out_specs=pl.BlockSpec((d, t), lambda i, e: (0, i)),
        out_shape=jax.ShapeDtypeStruct((d, n), F32),
        scratch_shapes=[pltpu.VMEM((PEER_ET, t), BF16)],
        compiler_params=_cparams(("parallel", "arbitrary")),
        name="peer_dense",
    )(u, vt, xnt, e2, r2, n1, w1)


def _add_t_kernel(x_ref, yt_ref, o_ref):
    o_ref[...] = x_ref[...] + yt_ref[...].T


def _add_t_norm_kernel(x_ref, yt_ref, g_ref, o_ref):
    o_ref[...] = _rms(x_ref[...] + yt_ref[...].T, g_ref[...])


def _add_transposed(x, yt, gain=None, tm=512):
    n, d = x.shape
    tm = min(tm, n)
    in_specs = [pl.BlockSpec((tm, d), lambda i: (i, 0)),
                pl.BlockSpec((d, tm), lambda i: (0, i))]
    args = [x, yt]
    body = _add_t_kernel
    if gain is not None:
        in_specs.append(pl.BlockSpec((1, d), lambda i: (0, 0)))
        args.append(gain.reshape(1, d))
        body = _add_t_norm_kernel
    return pl.pallas_call(
        body,
        grid=(n // tm,),
        in_specs=in_specs,
        out_specs=pl.BlockSpec((tm, d), lambda i: (i, 0)),
        out_shape=jax.ShapeDtypeStruct((n, d), F32),
        compiler_params=_cparams(("parallel",)),
        name="add_transposed",
    )(*args)


def _swap_halves(w):
    half = w.shape[-1] // 2
    return jnp.concatenate([w[..., half:], w[..., :half]], axis=-1)


def _prep_w_in(w_in):
    d = w_in.shape[0]
    kr = w_in[:, KR_OFF:D_IN]
    z = jnp.zeros((d, HEAD_DIM - C_ROPE), w_in.dtype)
    return jnp.concatenate([w_in[:, :KR_OFF], kr, z, _swap_halves(kr), z], axis=-1).astype(BF16)


def _prep_w_uq(w_uq):
    r = w_uq.shape[0]
    z = jnp.zeros((r, HEAD_DIM - C_ROPE), w_uq.dtype)
    blocks = []
    for h in range(C_HEADS):
        b = h * (C_NOPE + C_ROPE)
        rope = w_uq[:, b + C_NOPE:b + C_NOPE + C_ROPE]
        blocks += [w_uq[:, b:b + C_NOPE], rope, z, _swap_halves(rope), z]
    return jnp.concatenate(blocks, axis=-1).astype(BF16)


def _prep_w_ukv(w_ukv):
    ks = [w_ukv[:, h * (C_NOPE + C_V):h * (C_NOPE + C_V) + C_NOPE] for h in range(C_HEADS)]
    vs = [w_ukv[:, h * (C_NOPE + C_V) + C_NOPE:(h + 1) * (C_NOPE + C_V)] for h in range(C_HEADS)]
    return jnp.concatenate(ks + vs, axis=-1).astype(BF16)


def _mixer(x2, bsz, s_len, ln1, w_in, a_sink, b_rel_bias, c_q_norm, c_kv_norm, c_w_uq, c_w_ukv,
           out_norm, w_o):
    h2 = _norm_matmul(x2, ln1, _prep_w_in(w_in))
    h3 = h2.reshape(bsz, s_len, D_IN_PAD)
    oa = _attn_a(h3, a_sink.astype(F32), out_norm[:A_Q])
    ob = _attn_b(h3, b_rel_bias, out_norm[A_Q:A_Q + B_W])
    qc, kc, vc = _mla_proj(h2, s_len, c_q_norm, c_kv_norm, _prep_w_uq(c_w_uq), _prep_w_ukv(c_w_ukv))
    oc = _mla_attn(qc.reshape(bsz, s_len, -1), kc.reshape(bsz, s_len, -1),
                   vc.reshape(bsz, s_len, -1), out_norm[A_Q + B_W:])
    n = bsz * s_len
    return _out_proj(x2, oa.reshape(n, A_Q), ob.reshape(n, B_W), oc.reshape(n, C_OUT),
                     w_o.astype(BF16))


def _peer(x2, ln2, w_q, sub_keys, u, vv):
    xnt, e2, r2, n1, w1 = _peer_route(x2, ln2, w_q.T.astype(BF16), sub_keys.astype(F32))
    return _peer_dense(u.astype(BF16), vv.T.astype(BF16), xnt, e2, r2, n1, w1)


def kernel(x, ln1, w_in, a_sink, b_rel_bias, c_q_norm, c_kv_norm, c_w_uq, c_w_ukv, out_norm, w_o,
           ln2, peer_w_q, peer_sub_keys, peer_u, peer_v, final_norm):
    bsz, s_len, d = x.shape
    depth = ln1.shape[0]
    x2 = x.reshape(bsz * s_len, d)
    for l in range(depth):
        x2 = _mixer(x2, bsz, s_len, ln1[l], w_in[l], a_sink[l], b_rel_bias[l], c_q_norm[l],
                    c_kv_norm[l], c_w_uq[l], c_w_ukv[l], out_norm[l], w_o[l])
        yt = _peer(x2, ln2[l], peer_w_q[l], peer_sub_keys[l], peer_u[l], peer_v[l])
        x2 = _add_transposed(x2, yt, final_norm if l == depth - 1 else None)
    return x2.reshape(bsz, s_len, d)
```

```python
import functools
import math

import jax
import jax.numpy as jnp
import numpy as np
from jax import lax
from jax.experimental import pallas as pl
from jax.experimental.pallas import tpu as pltpu

F32 = jnp.float32
BF16 = jnp.bfloat16

EPS = 1e-6
HEAD_DIM = 128
A_HEADS = 8
A_KV_HEADS = 2
A_GROUP = A_HEADS // A_KV_HEADS
A_WINDOW = 128
B_HEADS = 4
GRID_W = 64
NB_ROWS = 8
NB_COLS = 16
C_HEADS = 4
C_Q_RANK = 512
C_KV_RANK = 256
C_NOPE = 128
C_ROPE = 64
C_V = 128
ROPE_THETA = 10000.0
A_Q = A_HEADS * HEAD_DIM
A_KV = A_KV_HEADS * HEAD_DIM
B_W = B_HEADS * HEAD_DIM
C_OUT = C_HEADS * C_V
D_MIX = A_Q + B_W + C_OUT
D_IN = A_Q + 2 * A_KV + 3 * B_W + C_Q_RANK + C_KV_RANK + C_ROPE
D_IN_PAD = 4096
KR_OFF = D_IN - C_ROPE
PEER_HEADS = 8
PEER_KEYS = 128
PEER_HALF = 64
PEER_TOPK = 16

VMEM_LIMIT = 56 * 1024 * 1024
NEG_INF = float("-inf")


def _cparams(sem):
    return pltpu.CompilerParams(dimension_semantics=sem, vmem_limit_bytes=VMEM_LIMIT)


def _rms(x, g):
    ms = jnp.mean(x * x, axis=-1, keepdims=True)
    return x * lax.rsqrt(ms + EPS) * g


def _norm_matmul_kernel(x_ref, g_ref, w_ref, o_ref, xn_ref):
    @pl.when(pl.program_id(1) == 0)
    def _():
        xn_ref[...] = _rms(x_ref[...], g_ref[...]).astype(BF16)

    o_ref[...] = jnp.dot(xn_ref[...], w_ref[...],
                         preferred_element_type=F32).astype(o_ref.dtype)


def _norm_matmul(x, g, w, tm=1024, tn=512):
    n, d = x.shape
    nout = w.shape[1]
    tm = min(tm, n)
    return pl.pallas_call(
        _norm_matmul_kernel,
        grid=(n // tm, nout // tn),
        in_specs=[pl.BlockSpec((tm, d), lambda i, j: (i, 0)),
                  pl.BlockSpec((1, d), lambda i, j: (0, 0)),
                  pl.BlockSpec((d, tn), lambda i, j: (0, j))],
        out_specs=pl.BlockSpec((tm, tn), lambda i, j: (i, j)),
        out_shape=jax.ShapeDtypeStruct((n, nout), BF16),
        scratch_shapes=[pltpu.VMEM((tm, d), BF16)],
        compiler_params=_cparams(("parallel", "arbitrary")),
        name="norm_matmul",
    )(x, g.reshape(1, d), w)


A_TQ = 512
A_BAND = 3 * A_WINDOW


def _attn_a_kernel(sink_ref, q_ref, k_ref, v_ref, g_ref, o_ref):
    i = pl.program_id(1)
    s_len = k_ref.shape[0]
    scale = HEAD_DIM ** -0.5
    slopes = [2.0 ** (-8.0 * (h + 1) / A_HEADS) for h in range(A_HEADS)]
    for j in range(A_TQ // A_WINDOW):
        q0 = i * A_TQ + j * A_WINDOW
        start = pl.multiple_of(jnp.clip(q0 - A_WINDOW, 0, s_len - A_BAND), A_WINDOW)
        kb = k_ref[pl.ds(start, A_BAND), :]
        vb = v_ref[pl.ds(start, A_BAND), :]
        q_pos = q0 + lax.broadcasted_iota(jnp.int32, (A_WINDOW, A_BAND), 0)
        k_pos = start + lax.broadcasted_iota(jnp.int32, (A_WINDOW, A_BAND), 1)
        dist = jnp.abs(q_pos - k_pos)
        valid = dist <= A_WINDOW
        dist_f = dist.astype(F32)
        outs = []
        ssq = jnp.zeros((A_WINDOW, 1), F32)
        for kh in range(A_KV_HEADS):
            kk = kb[:, kh * HEAD_DIM:(kh + 1) * HEAD_DIM]
            vv = vb[:, kh * HEAD_DIM:(kh + 1) * HEAD_DIM]
            for g in range(A_GROUP):
                h = kh * A_GROUP + g
                qh = q_ref[j * A_WINDOW:(j + 1) * A_WINDOW, h * HEAD_DIM:(h + 1) * HEAD_DIM]
                s = lax.dot_general(qh, kk, (((1,), (1,)), ((), ())),
                                    preferred_element_type=F32) * scale
                s = s - slopes[h] * dist_f
                s = jnp.where(valid, s, NEG_INF)
                sink = sink_ref[h]
                m = jnp.maximum(jnp.max(s, axis=-1, keepdims=True), sink)
                p = jnp.exp(s - m)
                denom = jnp.sum(p, axis=-1, keepdims=True) + jnp.exp(sink - m)
                p = p / denom
                o = jnp.dot(p.astype(BF16), vv, preferred_element_type=F32)
                ssq = ssq + jnp.sum(o * o, axis=-1, keepdims=True)
                outs.append(o)
        inv = lax.rsqrt(ssq / A_Q + EPS)
        for h in range(A_HEADS):
            cols = slice(h * HEAD_DIM, (h + 1) * HEAD_DIM)
            o_ref[j * A_WINDOW:(j + 1) * A_WINDOW, cols] = (
                outs[h] * inv * g_ref[:, cols]).astype(o_ref.dtype)


def _attn_a(h3, sink, gain):
    bsz, s_len, _ = h3.shape
    assert s_len % A_TQ == 0 and s_len >= A_BAND
    kcol = A_Q // A_KV
    return pl.pallas_call(
        _attn_a_kernel,
        grid=(bsz, s_len // A_TQ),
        in_specs=[pl.BlockSpec(memory_space=pltpu.SMEM),
                  pl.BlockSpec((None, A_TQ, A_Q), lambda b, i: (b, i, 0)),
                  pl.BlockSpec((None, s_len, A_KV), lambda b, i: (b, 0, kcol)),
                  pl.BlockSpec((None, s_len, A_KV), lambda b, i: (b, 0, kcol + 1)),
                  pl.BlockSpec((1, A_Q), lambda b, i: (0, 0))],
        out_specs=pl.BlockSpec((None, A_TQ, A_Q), lambda b, i: (b, i, 0)),
        out_shape=jax.ShapeDtypeStruct((bsz, s_len, A_Q), BF16),
        compiler_params=_cparams(("parallel", "arbitrary")),
        name="attn_a",
    )(sink, h3, h3, h3, gain.reshape(1, A_Q))


B_RQ = 8
B_KEYS = NB_ROWS * GRID_W
B_QCOL = (A_Q + 2 * A_KV) // B_W


def _attn_b_kernel(q_ref, k_ref, v_ref, bias_ref, g_ref, o_ref):
    i = pl.program_id(1)
    rows = k_ref.shape[0] // GRID_W
    scale = HEAD_DIM ** -0.5
    for jr in range(B_RQ):
        r = i * B_RQ + jr
        r_start = jnp.clip(r - NB_ROWS // 2, 0, rows - NB_ROWS)
        off = r_start - r + NB_ROWS - 1
        kstart = pl.multiple_of(r_start * GRID_W, GRID_W)
        kb = k_ref[pl.ds(kstart, B_KEYS), :]
        vb = v_ref[pl.ds(kstart, B_KEYS), :]
        outs = []
        ssq = jnp.zeros((GRID_W, 1), F32)
        for h in range(B_HEADS):
            cols = slice(h * HEAD_DIM, (h + 1) * HEAD_DIM)
            qh = q_ref[jr * GRID_W:(jr + 1) * GRID_W, cols]
            s = lax.dot_general(qh, kb[:, cols], (((1,), (1,)), ((), ())),
                                preferred_element_type=F32) * scale
            s = s + bias_ref[off, h]
            m = jnp.max(s, axis=-1, keepdims=True)
            p = jnp.exp(s - m)
            p = p / jnp.sum(p, axis=-1, keepdims=True)
            o = jnp.dot(p.astype(BF16), vb[:, cols], preferred_element_type=F32)
            ssq = ssq + jnp.sum(o * o, axis=-1, keepdims=True)
            outs.append(o)
        inv = lax.rsqrt(ssq / B_W + EPS)
        for h in range(B_HEADS):
            cols = slice(h * HEAD_DIM, (h + 1) * HEAD_DIM)
            o_ref[jr * GRID_W:(jr + 1) * GRID_W, cols] = (
                outs[h] * inv * g_ref[:, cols]).astype(o_ref.dtype)


def _nb_bias_table(rel_bias):
    col = jnp.arange(GRID_W)
    c_start = jnp.clip(col - NB_COLS // 2, 0, GRID_W - NB_COLS)
    col_ok = (col[None, :] >= c_start[:, None]) & (col[None, :] < c_start[:, None] + NB_COLS)
    dc_idx = jnp.clip(col[None, :] - col[:, None] + NB_COLS - 1, 0, 2 * NB_COLS - 2)
    rb = rel_bias.astype(F32)[:, :, dc_idx]
    rb = jnp.where(col_ok[None, None], rb, NEG_INF)
    tabs = []
    for off in range(NB_ROWS):
        t = jnp.transpose(rb[:, off:off + NB_ROWS], (0, 2, 1, 3))
        tabs.append(t.reshape(B_HEADS, GRID_W, B_KEYS))
    return jnp.stack(tabs)


def _attn_b(h3, rel_bias, gain):
    bsz, s_len, _ = h3.shape
    rows = s_len // GRID_W
    assert rows >= NB_ROWS and rows % B_RQ == 0
    tq = B_RQ * GRID_W
    bias = _nb_bias_table(rel_bias)
    return pl.pallas_call(
        _attn_b_kernel,
        grid=(bsz, rows // B_RQ),
        in_specs=[pl.BlockSpec((None, tq, B_W), lambda b, i: (b, i, B_QCOL)),
                  pl.BlockSpec((None, s_len, B_W), lambda b, i: (b, 0, B_QCOL + 1)),
                  pl.BlockSpec((None, s_len, B_W), lambda b, i: (b, 0, B_QCOL + 2)),
                  pl.BlockSpec((NB_ROWS, B_HEADS, GRID_W, B_KEYS), lambda b, i: (0, 0, 0, 0)),
                  pl.BlockSpec((1, B_W), lambda b, i: (0, 0))],
        out_specs=pl.BlockSpec((None, tq, B_W), lambda b, i: (b, i, 0)),
        out_shape=jax.ShapeDtypeStruct((bsz, s_len, B_W), BF16),
        compiler_params=_cparams(("parallel", "arbitrary")),
        name="attn_b",
    )(h3, h3, h3, bias, gain.reshape(1, B_W))


C_QK = 256
C_QW = 3 * HEAD_DIM
MLA_TM = 512
MLA_TQ = 256


def _mla_proj_kernel(cq_ref, ckv_ref, kr_ref, gq_ref, gkv_ref, wq_ref, wkv_ref,
                     cos_ref, sin_ref, q_ref, k_ref, v_ref):
    cos = cos_ref[...]
    sin = sin_ref[...]
    cqn = _rms(cq_ref[...].astype(F32), gq_ref[...]).astype(BF16)
    qf = jnp.dot(cqn, wq_ref[...], preferred_element_type=F32)
    ckvn = _rms(ckv_ref[...].astype(F32), gkv_ref[...]).astype(BF16)
    kvf = jnp.dot(ckvn, wkv_ref[...], preferred_element_type=F32)
    kr = kr_ref[...].astype(F32)
    kpe = (kr[:, :HEAD_DIM] * cos + kr[:, HEAD_DIM:] * sin).astype(k_ref.dtype)
    for h in range(C_HEADS):
        b = h * C_QW
        q_ref[:, h * C_QK:h * C_QK + C_NOPE] = qf[:, b:b + C_NOPE].astype(q_ref.dtype)
        rope = qf[:, b + HEAD_DIM:b + 2 * HEAD_DIM] * cos + qf[:, b + 2 * HEAD_DIM:b + 3 * HEAD_DIM] * sin
        q_ref[:, h * C_QK + C_NOPE:(h + 1) * C_QK] = rope.astype(q_ref.dtype)
        k_ref[:, h * C_QK:h * C_QK + C_NOPE] = kvf[:, h * C_NOPE:(h + 1) * C_NOPE].astype(k_ref.dtype)
        k_ref[:, h * C_QK + C_NOPE:(h + 1) * C_QK] = kpe
    v_ref[...] = kvf[:, C_HEADS * C_NOPE:].astype(v_ref.dtype)


def _rope_tables(s_len):
    half = C_ROPE // 2
    inv = ROPE_THETA ** (-jnp.arange(half, dtype=F32) / half)
    ang = jnp.arange(s_len, dtype=F32)[:, None] * inv[None, :]
    cos = jnp.cos(ang)
    sin = jnp.sin(ang)
    zeros = jnp.zeros((s_len, HEAD_DIM - C_ROPE), F32)
    return (jnp.concatenate([cos, cos, zeros], axis=-1),
            jnp.concatenate([-sin, sin, zeros], axis=-1))


def _mla_proj(h2, s_len, gq, gkv, wq, wkv):
    n = h2.shape[0]
    tm = min(MLA_TM, s_len)
    cos, sin = _rope_tables(s_len)
    per_seq = s_len // tm
    qk_shape = jax.ShapeDtypeStruct((n, C_HEADS * C_QK), BF16)
    cq_col = (D_IN - C_ROPE - C_KV_RANK - C_Q_RANK) // C_Q_RANK
    ckv_col = (D_IN - C_ROPE - C_KV_RANK) // C_KV_RANK
    kr_col = KR_OFF // (2 * HEAD_DIM)
    return pl.pallas_call(
        _mla_proj_kernel,
        grid=(n // tm,),
        in_specs=[pl.BlockSpec((tm, C_Q_RANK), lambda i: (i, cq_col)),
                  pl.BlockSpec((tm, C_KV_RANK), lambda i: (i, ckv_col)),
                  pl.BlockSpec((tm, 2 * HEAD_DIM), lambda i: (i, kr_col)),
                  pl.BlockSpec((1, C_Q_RANK), lambda i: (0, 0)),
                  pl.BlockSpec((1, C_KV_RANK), lambda i: (0, 0)),
                  pl.BlockSpec(wq.shape, lambda i: (0, 0)),
                  pl.BlockSpec(wkv.shape, lambda i: (0, 0)),
                  pl.BlockSpec((tm, HEAD_DIM), lambda i: (i % per_seq, 0)),
                  pl.BlockSpec((tm, HEAD_DIM), lambda i: (i % per_seq, 0))],
        out_specs=[pl.BlockSpec((tm, C_HEADS * C_QK), lambda i: (i, 0)),
                   pl.BlockSpec((tm, C_HEADS * C_QK), lambda i: (i, 0)),
                   pl.BlockSpec((tm, C_OUT), lambda i: (i, 0))],
        out_shape=[qk_shape, qk_shape, jax.ShapeDtypeStruct((n, C_OUT), BF16)],
        compiler_params=_cparams(("parallel",)),
        name="mla_proj",
    )(h2, h2, h2, gq.reshape(1, -1), gkv.reshape(1, -1), wq, wkv, cos, sin)


def _mla_attn_kernel(q_ref, k_ref, v_ref, g_ref, o_ref):
    scale = (C_NOPE + C_ROPE) ** -0.5
    outs = []
    ssq = jnp.zeros((q_ref.shape[0], 1), F32)
    for h in range(C_HEADS):
        qk = slice(h * C_QK, (h + 1) * C_QK)
        s = lax.dot_general(q_ref[:, qk], k_ref[:, qk], (((1,), (1,)), ((), ())),
                            preferred_element_type=F32) * scale
        m = jnp.max(s, axis=-1, keepdims=True)
        p = jnp.exp(s - m)
        p = p / jnp.sum(p, axis=-1, keepdims=True)
        o = jnp.dot(p.astype(BF16), v_ref[:, h * C_V:(h + 1) * C_V],
                    preferred_element_type=F32)
        ssq = ssq + jnp.sum(o * o, axis=-1, keepdims=True)
        outs.append(o)
    inv = lax.rsqrt(ssq / C_OUT + EPS)
    for h in range(C_HEADS):
        cols = slice(h * C_V, (h + 1) * C_V)
        o_ref[:, cols] = (outs[h] * inv * g_ref[:, cols]).astype(o_ref.dtype)


def _mla_attn(q3, k3, v3, gain):
    bsz, s_len, _ = q3.shape
    tq = min(MLA_TQ, s_len)
    return pl.pallas_call(
        _mla_attn_kernel,
        grid=(bsz, s_len // tq),
        in_specs=[pl.BlockSpec((None, tq, C_HEADS * C_QK), lambda b, i: (b, i, 0)),
                  pl.BlockSpec((None, s_len, C_HEADS * C_QK), lambda b, i: (b, 0, 0)),
                  pl.BlockSpec((None, s_len, C_OUT), lambda b, i: (b, 0, 0)),
                  pl.BlockSpec((1, C_OUT), lambda b, i: (0, 0))],
        out_specs=pl.BlockSpec((None, tq, C_OUT), lambda b, i: (b, i, 0)),
        out_shape=jax.ShapeDtypeStruct((bsz, s_len, C_OUT), BF16),
        compiler_params=_cparams(("parallel", "arbitrary")),
        name="mla_attn",
    )(q3, k3, v3, gain.reshape(1, C_OUT))


def _out_proj_kernel(x_ref, oa_ref, ob_ref, oc_ref, wa_ref, wb_ref, wc_ref, o_ref):
    acc = jnp.dot(oa_ref[...], wa_ref[...], preferred_element_type=F32)
    acc = acc + jnp.dot(ob_ref[...], wb_ref[...], preferred_element_type=F32)
    acc = acc + jnp.dot(oc_ref[...], wc_ref[...], preferred_element_type=F32)
    o_ref[...] = x_ref[...] + acc


def _out_proj(x, oa, ob, oc, w_o, tm=1024, tn=512):
    n, d = x.shape
    tm = min(tm, n)
    return pl.pallas_call(
        _out_proj_kernel,
        grid=(n // tm, d // tn),
        in_specs=[pl.BlockSpec((tm, tn), lambda i, j: (i, j)),
                  pl.BlockSpec((tm, A_Q), lambda i, j: (i, 0)),
                  pl.BlockSpec((tm, B_W), lambda i, j: (i, 0)),
                  pl.BlockSpec((tm, C_OUT), lambda i, j: (i, 0)),
                  pl.BlockSpec((A_Q, tn), lambda i, j: (0, j)),
                  pl.BlockSpec((B_W, tn), lambda i, j: (A_Q // B_W, j)),
                  pl.BlockSpec((C_OUT, tn), lambda i, j: ((A_Q + B_W) // C_OUT, j))],
        out_specs=pl.BlockSpec((tm, tn), lambda i, j: (i, j)),
        out_shape=jax.ShapeDtypeStruct((n, d), F32),
        compiler_params=_cparams(("parallel", "arbitrary")),
        name="out_proj",
    )(x, oa, ob, oc, w_o, w_o, w_o)


ROUTE_TM = 256
_CAND_ROWS = [(a, PEER_TOPK // (a + 1)) for a in range(PEER_TOPK)]
SUBLANES = 8


def _top_ranks(s):
    idx = lax.broadcasted_iota(jnp.int32, s.shape, 0).astype(F32)
    rank = jnp.full(s.shape, float(PEER_KEYS), F32)
    vals = []
    for k in range(PEER_TOPK):
        m = jnp.max(s, axis=0, keepdims=True)
        first = jnp.min(jnp.where(s == m, idx, float(PEER_KEYS)), axis=0, keepdims=True)
        hit = idx == first
        rank = jnp.where(hit, float(k), rank)
        s = jnp.where(hit, NEG_INF, s)
        vals.append(m)
    return rank, vals


def _bf16_bits_high(x):
    return pltpu.bitcast(x.astype(BF16).astype(F32), jnp.uint32)


def _pack_halves(x):
    half = x.shape[0] // 2
    return (_bf16_bits_high(x[:half]) >> 16) | _bf16_bits_high(x[half:])


def _pack_twice(x):
    bits = _bf16_bits_high(x)
    return (bits >> 16) | bits


def _packed_order(w):
    n_exp, d = w.shape
    w = w.reshape(n_exp // PEER_KEYS, 2, PEER_KEYS // 2, d)
    return jnp.swapaxes(w, 1, 2).reshape(n_exp, d)


def _peer_route_kernel(x_ref, g_ref, wq_ref, keys_ref, xnt_ref, e2_ref, r2_ref, n1_ref, w1_ref):
    tm = x_ref.shape[0]
    xn = _rms(x_ref[...], g_ref[...])
    xnt = xn.T.astype(BF16)
    xnt_ref[...] = xnt
    qt = jnp.dot(wq_ref[...], xnt, preferred_element_type=F32)
    for h in range(PEER_HEADS):
        sc = []
        for c in range(2):
            row0 = (h * 2 + c) * PEER_HALF
            sc.append(jnp.dot(keys_ref[c], qt[row0:row0 + PEER_HALF, :],
                              preferred_element_type=F32,
                              precision=lax.Precision.HIGHEST))
        rank1, t1 = _top_ranks(sc[0])
        rank2, t2 = _top_ranks(sc[1])
        e1 = [jnp.exp(t1[a] - t1[0]) for a in range(PEER_TOPK)]
        e2 = [jnp.exp(t2[b] - t2[0]) for b in range(PEER_TOPK)]
        groups, gmeta = [], []
        for a, nb in _CAND_ROWS:
            for b0 in range(0, nb, SUBLANES):
                bs = list(range(b0, min(b0 + SUBLANES, nb)))
                rows = [t1[a] + t2[b] for b in bs]
                rows += [jnp.full((1, tm), NEG_INF, F32)] * (SUBLANES - len(bs))
                groups.append(jnp.concatenate(rows, axis=0))
                gmeta.append((a, bs))
        cand = jnp.concatenate(groups, axis=0)
        cidx = lax.broadcasted_iota(jnp.int32, cand.shape, 0).astype(F32)
        self_f = jnp.zeros(cand.shape, F32)
        for _ in range(PEER_TOPK):
            m = jnp.max(cand, axis=0, keepdims=True)
            first = jnp.min(jnp.where(cand == m, cidx, float(cand.shape[0])), axis=0, keepdims=True)
            hit = cidx == first
            self_f = jnp.where(hit, 1.0, self_f)
            cand = jnp.where(hit, NEG_INF, cand)
        n_a = [jnp.zeros((1, tm), F32) for _ in range(PEER_TOPK)]
        z = jnp.zeros((1, tm), F32)
        for gi, (a, bs) in enumerate(gmeta):
            blk = self_f[gi * SUBLANES:(gi + 1) * SUBLANES, :]
            n_a[a] = n_a[a] + jnp.sum(blk, axis=0, keepdims=True)
            for bi, b in enumerate(bs):
                z = z + blk[bi:bi + 1, :] * (e1[a] * e2[b])
        n1 = jnp.zeros(rank1.shape, F32)
        for a in range(PEER_TOPK):
            n1 = jnp.where(rank1 == float(a), n_a[a], n1)
        e2_ref[h] = _pack_halves(jnp.exp(sc[1] - t2[0]))
        r2_ref[h] = _pack_halves(rank2)
        n1_ref[h] = _pack_twice(n1).reshape(PEER_KEYS // SUBLANES, SUBLANES, tm)
        w1_ref[h] = _pack_twice(jnp.exp(sc[0] - t1[0]) / z).reshape(PEER_KEYS // SUBLANES, SUBLANES, tm)


def _peer_route(x, g, wq_t, sub_keys):
    n, d = x.shape
    tm = min(ROUTE_TM, n)
    plane = jax.ShapeDtypeStruct((PEER_HEADS, PEER_KEYS // 2, n), jnp.uint32)
    plane_spec = pl.BlockSpec((PEER_HEADS, PEER_KEYS // 2, tm), lambda i: (0, 0, i))
    table = jax.ShapeDtypeStruct((PEER_HEADS, PEER_KEYS // SUBLANES, SUBLANES, n), jnp.uint32)
    table_spec = pl.BlockSpec((PEER_HEADS, PEER_KEYS // SUBLANES, SUBLANES, tm),
                              lambda i: (0, 0, 0, i))
    return pl.pallas_call(
        _peer_route_kernel,
        grid=(n // tm,),
        in_specs=[pl.BlockSpec((tm, d), lambda i: (i, 0)),
                  pl.BlockSpec((1, d), lambda i: (0, 0)),
                  pl.BlockSpec(wq_t.shape, lambda i: (0, 0)),
                  pl.BlockSpec(sub_keys.shape, lambda i: (0, 0, 0))],
        out_specs=[pl.BlockSpec((d, tm), lambda i: (0, i)),
                   plane_spec, plane_spec, table_spec, table_spec],
        out_shape=[jax.ShapeDtypeStruct((d, n), BF16), plane, plane, table, table],
        compiler_params=_cparams(("parallel",)),
        name="peer_route",
    )(x, g.reshape(1, d), wq_t, sub_keys)


PEER_T = 512
PEER_ET = SUBLANES * PEER_KEYS
LANES = 128


def _rows_bf16(words):
    return pltpu.bitcast(jnp.broadcast_to(words, (PEER_KEYS // 2, LANES)), BF16)


def _peer_dense_kernel(u_ref, vt_ref, xnt_ref, e2_ref, r2_ref, n1_ref, w1_ref, yt_ref, h_ref):
    e = pl.program_id(1)
    t = xnt_ref.shape[1]

    @pl.when(e == 0)
    def _():
        yt_ref[...] = jnp.zeros_like(yt_ref)

    at = jnp.dot(u_ref[...], xnt_ref[...], preferred_element_type=F32)
    for rl in range(SUBLANES):
        for tc in range(t // LANES):
            cols = slice(tc * LANES, (tc + 1) * LANES)
            gate = jnp.zeros((PEER_KEYS, LANES), BF16)
            for h in range(PEER_HEADS):
                n1r = _rows_bf16(n1_ref[h, e, rl:rl + 1, cols])
                w1r = _rows_bf16(w1_ref[h, e, rl:rl + 1, cols])
                r2 = pltpu.bitcast(r2_ref[h, :, cols], BF16)
                e2 = pltpu.bitcast(e2_ref[h, :, cols], BF16)
                gate = gate + jnp.where(r2 < n1r, e2 * w1r, jnp.zeros((), BF16))
            a = at[rl * PEER_KEYS:(rl + 1) * PEER_KEYS, cols]
            h_ref[rl * PEER_KEYS:(rl + 1) * PEER_KEYS, cols] = jax.nn.gelu(a.astype(BF16)) * gate
    yt_ref[...] += jnp.dot(vt_ref[...], h_ref[...], preferred_element_type=F32)


def _peer_dense(u, vt, xnt, e2, r2, n1, w1):
    n_exp, d = u.shape
    n = xnt.shape[1]
    t = min(PEER_T, n)
    plane_spec = pl.BlockSpec((PEER_HEADS, PEER_KEYS // 2, t), lambda i, e: (0, 0, i))
    table_spec = pl.BlockSpec((PEER_HEADS, PEER_KEYS // SUBLANES, SUBLANES, t),
                              lambda i, e: (0, 0, 0, i))
    return pl.pallas_call(
        _peer_dense_kernel,
        grid=(n // t, n_exp // PEER_ET),
        in_specs=[pl.BlockSpec((PEER_ET, d), lambda i, e: (e, 0)),
                  pl.BlockSpec((d, PEER_ET), lambda i, e: (0, e)),
                  pl.BlockSpec((d, t), lambda i, e: (0, i)),
                  plane_spec, plane_spec, table_spec, table_spec],
        out_specs=pl.BlockSpec((d, t), lambda i, e: (0, i)),
        out_shape=jax.ShapeDtypeStruct((d, n), F32),
        scratch_shapes=[pltpu.VMEM((PEER_ET, t), BF16)],
        compiler_params=_cparams(("parallel", "arbitrary")),
        name="peer_dense",
    )(u, vt, xnt, e2, r2, n1, w1)


def _add_t_kernel(x_ref, yt_ref, o_ref):
    o_ref[...] = x_ref[...] + yt_ref[...].T


def _add_t_norm_kernel(x_ref, yt_ref, g_ref, o_ref):
    o_ref[...] = _rms(x_ref[...] + yt_ref[...].T, g_ref[...])


def _add_transposed(x, yt, gain=None, tm=512):
    n, d = x.shape
    tm = min(tm, n)
    in_specs = [pl.BlockSpec((tm, d), lambda i: (i, 0)),
                pl.BlockSpec((d, tm), lambda i: (0, i))]
    args = [x, yt]
    body = _add_t_kernel
    if gain is not None:
        in_specs.append(pl.BlockSpec((1, d), lambda i: (0, 0)))
        args.append(gain.reshape(1, d))
        body = _add_t_norm_kernel
    return pl.pallas_call(
        body,
        grid=(n // tm,),
        in_specs=in_specs,
        out_specs=pl.BlockSpec((tm, d), lambda i: (i, 0)),
        out_shape=jax.ShapeDtypeStruct((n, d), F32),
        compiler_params=_cparams(("parallel",)),
        name="add_transposed",
    )(*args)


def _swap_halves(w):
    half = w.shape[-1] // 2
    return jnp.concatenate([w[..., half:], w[..., :half]], axis=-1)


def _prep_w_in(w_in):
    d = w_in.shape[0]
    kr = w_in[:, KR_OFF:D_IN]
    z = jnp.zeros((d, HEAD_DIM - C_ROPE), w_in.dtype)
    return jnp.concatenate([w_in[:, :KR_OFF], kr, z, _swap_halves(kr), z], axis=-1).astype(BF16)


def _prep_w_uq(w_uq):
    r = w_uq.shape[0]
    z = jnp.zeros((r, HEAD_DIM - C_ROPE), w_uq.dtype)
    blocks = []
    for h in range(C_HEADS):
        b = h * (C_NOPE + C_ROPE)
        rope = w_uq[:, b + C_NOPE:b + C_NOPE + C_ROPE]
        blocks += [w_uq[:, b:b + C_NOPE], rope, z, _swap_halves(rope), z]
    return jnp.concatenate(blocks, axis=-1).astype(BF16)


def _prep_w_ukv(w_ukv):
    ks = [w_ukv[:, h * (C_NOPE + C_V):h * (C_NOPE + C_V) + C_NOPE] for h in range(C_HEADS)]
    vs = [w_ukv[:, h * (C_NOPE + C_V) + C_NOPE:(h + 1) * (C_NOPE + C_V)] for h in range(C_HEADS)]
    return jnp.concatenate(ks + vs, axis=-1).astype(BF16)


def _mixer(x2, bsz, s_len, ln1, w_in, a_sink, b_rel_bias, c_q_norm, c_kv_norm, c_w_uq, c_w_ukv,
           out_norm, w_o):
    h2 = _norm_matmul(x2, ln1, _prep_w_in(w_in))
    h3 = h2.reshape(bsz, s_len, D_IN_PAD)
    oa = _attn_a(h3, a_sink.astype(F32), out_norm[:A_Q])
    ob = _attn_b(h3, b_rel_bias, out_norm[A_Q:A_Q + B_W])
    qc, kc, vc = _mla_proj(h2, s_len, c_q_norm, c_kv_norm, _prep_w_uq(c_w_uq), _prep_w_ukv(c_w_ukv))
    oc = _mla_attn(qc.reshape(bsz, s_len, -1), kc.reshape(bsz, s_len, -1),
                   vc.reshape(bsz, s_len, -1), out_norm[A_Q + B_W:])
    n = bsz * s_len
    return _out_proj(x2, oa.reshape(n, A_Q), ob.reshape(n, B_W), oc.reshape(n, C_OUT),
                     w_o.astype(BF16))


def _peer(x2, ln2, w_q, sub_keys, u, vv):
    xnt, e2, r2, n1, w1 = _peer_route(x2, ln2, w_q.T.astype(BF16), sub_keys.astype(F32))
    return _peer_dense(_packed_order(u).astype(BF16), _packed_order(vv).T.astype(BF16),
                       xnt, e2, r2, n1, w1)


def kernel(x, ln1, w_in, a_sink, b_rel_bias, c_q_norm, c_kv_norm, c_w_uq, c_w_ukv, out_norm, w_o,
           ln2, peer_w_q, peer_sub_keys, peer_u, peer_v, final_norm):
    bsz, s_len, d = x.shape
    depth = ln1.shape[0]
    x2 = x.reshape(bsz * s_len, d)
    for l in range(depth):
        x2 = _mixer(x2, bsz, s_len, ln1[l], w_in[l], a_sink[l], b_rel_bias[l], c_q_norm[l],
                    c_kv_norm[l], c_w_uq[l], c_w_ukv[l], out_norm[l], w_o[l])
        yt = _peer(x2, ln2[l], peer_w_q[l], peer_sub_keys[l], peer_u[l], peer_v[l])
        x2 = _add_transposed(x2, yt, final_norm if l == depth - 1 else None)
    return x2.reshape(bsz, s_len, d)
```

```python
import functools
import math

import jax
import jax.numpy as jnp
import numpy as np
from jax import lax
from jax.experimental import pallas as pl
from jax.experimental.pallas import tpu as pltpu

F32 = jnp.float32
BF16 = jnp.bfloat16

EPS = 1e-6
HEAD_DIM = 128
A_HEADS = 8
A_KV_HEADS = 2
A_GROUP = A_HEADS // A_KV_HEADS
A_WINDOW = 128
B_HEADS = 4
GRID_W = 64
NB_ROWS = 8
NB_COLS = 16
C_HEADS = 4
C_Q_RANK = 512
C_KV_RANK = 256
C_NOPE = 128
C_ROPE = 64
C_V = 128
ROPE_THETA = 10000.0
A_Q = A_HEADS * HEAD_DIM
A_KV = A_KV_HEADS * HEAD_DIM
B_W = B_HEADS * HEAD_DIM
C_OUT = C_HEADS * C_V
D_MIX = A_Q + B_W + C_OUT
D_IN = A_Q + 2 * A_KV + 3 * B_W + C_Q_RANK + C_KV_RANK + C_ROPE
D_IN_PAD = 4096
KR_OFF = D_IN - C_ROPE
PEER_HEADS = 8
PEER_KEYS = 128
PEER_HALF = 64
PEER_TOPK = 16

VMEM_LIMIT = 56 * 1024 * 1024
NEG_INF = float("-inf")


def _cparams(sem):
    return pltpu.CompilerParams(dimension_semantics=sem, vmem_limit_bytes=VMEM_LIMIT)


def _rms(x, g):
    ms = jnp.mean(x * x, axis=-1, keepdims=True)
    return x * lax.rsqrt(ms + EPS) * g


def _norm_matmul_kernel(x_ref, g_ref, w_ref, o_ref, xn_ref):
    @pl.when(pl.program_id(1) == 0)
    def _():
        xn_ref[...] = _rms(x_ref[...], g_ref[...]).astype(BF16)

    o_ref[...] = jnp.dot(xn_ref[...], w_ref[...],
                         preferred_element_type=F32).astype(o_ref.dtype)


def _norm_matmul(x, g, w, tm=1024, tn=512):
    n, d = x.shape
    nout = w.shape[1]
    tm = min(tm, n)
    return pl.pallas_call(
        _norm_matmul_kernel,
        grid=(n // tm, nout // tn),
        in_specs=[pl.BlockSpec((tm, d), lambda i, j: (i, 0)),
                  pl.BlockSpec((1, d), lambda i, j: (0, 0)),
                  pl.BlockSpec((d, tn), lambda i, j: (0, j))],
        out_specs=pl.BlockSpec((tm, tn), lambda i, j: (i, j)),
        out_shape=jax.ShapeDtypeStruct((n, nout), BF16),
        scratch_shapes=[pltpu.VMEM((tm, d), BF16)],
        compiler_params=_cparams(("parallel", "arbitrary")),
        name="norm_matmul",
    )(x, g.reshape(1, d), w)


A_TQ = 512
A_BAND = 3 * A_WINDOW


def _attn_a_kernel(sink_ref, q_ref, k_ref, v_ref, g_ref, o_ref):
    i = pl.program_id(1)
    s_len = k_ref.shape[0]
    scale = HEAD_DIM ** -0.5
    slopes = [2.0 ** (-8.0 * (h + 1) / A_HEADS) for h in range(A_HEADS)]
    for j in range(A_TQ // A_WINDOW):
        q0 = i * A_TQ + j * A_WINDOW
        start = pl.multiple_of(jnp.clip(q0 - A_WINDOW, 0, s_len - A_BAND), A_WINDOW)
        kb = k_ref[pl.ds(start, A_BAND), :]
        vb = v_ref[pl.ds(start, A_BAND), :]
        q_pos = q0 + lax.broadcasted_iota(jnp.int32, (A_WINDOW, A_BAND), 0)
        k_pos = start + lax.broadcasted_iota(jnp.int32, (A_WINDOW, A_BAND), 1)
        dist = jnp.abs(q_pos - k_pos)
        valid = dist <= A_WINDOW
        dist_f = dist.astype(F32)
        outs = []
        ssq = jnp.zeros((A_WINDOW, 1), F32)
        for kh in range(A_KV_HEADS):
            kk = kb[:, kh * HEAD_DIM:(kh + 1) * HEAD_DIM]
            vv = vb[:, kh * HEAD_DIM:(kh + 1) * HEAD_DIM]
            for g in range(A_GROUP):
                h = kh * A_GROUP + g
                qh = q_ref[j * A_WINDOW:(j + 1) * A_WINDOW, h * HEAD_DIM:(h + 1) * HEAD_DIM]
                s = lax.dot_general(qh, kk, (((1,), (1,)), ((), ())),
                                    preferred_element_type=F32) * scale
                s = s - slopes[h] * dist_f
                s = jnp.where(valid, s, NEG_INF)
                sink = sink_ref[h]
                m = jnp.maximum(jnp.max(s, axis=-1, keepdims=True), sink)
                p = jnp.exp(s - m)
                denom = jnp.sum(p, axis=-1, keepdims=True) + jnp.exp(sink - m)
                p = p / denom
                o = jnp.dot(p.astype(BF16), vv, preferred_element_type=F32)
                ssq = ssq + jnp.sum(o * o, axis=-1, keepdims=True)
                outs.append(o)
        inv = lax.rsqrt(ssq / A_Q + EPS)
        for h in range(A_HEADS):
            cols = slice(h * HEAD_DIM, (h + 1) * HEAD_DIM)
            o_ref[j * A_WINDOW:(j + 1) * A_WINDOW, cols] = (
                outs[h] * inv * g_ref[:, cols]).astype(o_ref.dtype)


def _attn_a(h3, sink, gain):
    bsz, s_len, _ = h3.shape
    assert s_len % A_TQ == 0 and s_len >= A_BAND
    kcol = A_Q // A_KV
    return pl.pallas_call(
        _attn_a_kernel,
        grid=(bsz, s_len // A_TQ),
        in_specs=[pl.BlockSpec(memory_space=pltpu.SMEM),
                  pl.BlockSpec((None, A_TQ, A_Q), lambda b, i: (b, i, 0)),
                  pl.BlockSpec((None, s_len, A_KV), lambda b, i: (b, 0, kcol)),
                  pl.BlockSpec((None, s_len, A_KV), lambda b, i: (b, 0, kcol + 1)),
                  pl.BlockSpec((1, A_Q), lambda b, i: (0, 0))],
        out_specs=pl.BlockSpec((None, A_TQ, A_Q), lambda b, i: (b, i, 0)),
        out_shape=jax.ShapeDtypeStruct((bsz, s_len, A_Q), BF16),
        compiler_params=_cparams(("parallel", "arbitrary")),
        name="attn_a",
    )(sink, h3, h3, h3, gain.reshape(1, A_Q))


B_RQ = 8
B_KEYS = NB_ROWS * GRID_W
B_QCOL = (A_Q + 2 * A_KV) // B_W


def _attn_b_kernel(q_ref, k_ref, v_ref, bias_ref, g_ref, o_ref):
    i = pl.program_id(1)
    rows = k_ref.shape[0] // GRID_W
    scale = HEAD_DIM ** -0.5
    for jr in range(B_RQ):
        r = i * B_RQ + jr
        r_start = jnp.clip(r - NB_ROWS // 2, 0, rows - NB_ROWS)
        off = r_start - r + NB_ROWS - 1
        kstart = pl.multiple_of(r_start * GRID_W, GRID_W)
        kb = k_ref[pl.ds(kstart, B_KEYS), :]
        vb = v_ref[pl.ds(kstart, B_KEYS), :]
        outs = []
        ssq = jnp.zeros((GRID_W, 1), F32)
        for h in range(B_HEADS):
            cols = slice(h * HEAD_DIM, (h + 1) * HEAD_DIM)
            qh = q_ref[jr * GRID_W:(jr + 1) * GRID_W, cols]
            s = lax.dot_general(qh, kb[:, cols], (((1,), (1,)), ((), ())),
                                preferred_element_type=F32) * scale
            s = s + bias_ref[off, h]
            m = jnp.max(s, axis=-1, keepdims=True)
            p = jnp.exp(s - m)
            p = p / jnp.sum(p, axis=-1, keepdims=True)
            o = jnp.dot(p.astype(BF16), vb[:, cols], preferred_element_type=F32)
            ssq = ssq + jnp.sum(o * o, axis=-1, keepdims=True)
            outs.append(o)
        inv = lax.rsqrt(ssq / B_W + EPS)
        for h in range(B_HEADS):
            cols = slice(h * HEAD_DIM, (h + 1) * HEAD_DIM)
            o_ref[jr * GRID_W:(jr + 1) * GRID_W, cols] = (
                outs[h] * inv * g_ref[:, cols]).astype(o_ref.dtype)


def _nb_bias_table(rel_bias):
    col = jnp.arange(GRID_W)
    c_start = jnp.clip(col - NB_COLS // 2, 0, GRID_W - NB_COLS)
    col_ok = (col[None, :] >= c_start[:, None]) & (col[None, :] < c_start[:, None] + NB_COLS)
    dc_idx = jnp.clip(col[None, :] - col[:, None] + NB_COLS - 1, 0, 2 * NB_COLS - 2)
    rb = rel_bias.astype(F32)[:, :, dc_idx]
    rb = jnp.where(col_ok[None, None], rb, NEG_INF)
    tabs = []
    for off in range(NB_ROWS):
        t = jnp.transpose(rb[:, off:off + NB_ROWS], (0, 2, 1, 3))
        tabs.append(t.reshape(B_HEADS, GRID_W, B_KEYS))
    return jnp.stack(tabs)


def _attn_b(h3, rel_bias, gain):
    bsz, s_len, _ = h3.shape
    rows = s_len // GRID_W
    assert rows >= NB_ROWS and rows % B_RQ == 0
    tq = B_RQ * GRID_W
    bias = _nb_bias_table(rel_bias)
    return pl.pallas_call(
        _attn_b_kernel,
        grid=(bsz, rows // B_RQ),
        in_specs=[pl.BlockSpec((None, tq, B_W), lambda b, i: (b, i, B_QCOL)),
                  pl.BlockSpec((None, s_len, B_W), lambda b, i: (b, 0, B_QCOL + 1)),
                  pl.BlockSpec((None, s_len, B_W), lambda b, i: (b, 0, B_QCOL + 2)),
                  pl.BlockSpec((NB_ROWS, B_HEADS, GRID_W, B_KEYS), lambda b, i: (0, 0, 0, 0)),
                  pl.BlockSpec((1, B_W), lambda b, i: (0, 0))],
        out_specs=pl.BlockSpec((None, tq, B_W), lambda b, i: (b, i, 0)),
        out_shape=jax.ShapeDtypeStruct((bsz, s_len, B_W), BF16),
        compiler_params=_cparams(("parallel", "arbitrary")),
        name="attn_b",
    )(h3, h3, h3, bias, gain.reshape(1, B_W))


C_QK = 256
C_QW = 3 * HEAD_DIM
MLA_TM = 512
MLA_TQ = 256


def _mla_proj_kernel(cq_ref, ckv_ref, kr_ref, gq_ref, gkv_ref, wq_ref, wkv_ref,
                     cos_ref, sin_ref, q_ref, k_ref, v_ref):
    cos = cos_ref[...]
    sin = sin_ref[...]
    cqn = _rms(cq_ref[...].astype(F32), gq_ref[...]).astype(BF16)
    qf = jnp.dot(cqn, wq_ref[...], preferred_element_type=F32)
    ckvn = _rms(ckv_ref[...].astype(F32), gkv_ref[...]).astype(BF16)
    kvf = jnp.dot(ckvn, wkv_ref[...], preferred_element_type=F32)
    kr = kr_ref[...].astype(F32)
    kpe = (kr[:, :HEAD_DIM] * cos + kr[:, HEAD_DIM:] * sin).astype(k_ref.dtype)
    for h in range(C_HEADS):
        b = h * C_QW
        q_ref[:, h * C_QK:h * C_QK + C_NOPE] = qf[:, b:b + C_NOPE].astype(q_ref.dtype)
        rope = qf[:, b + HEAD_DIM:b + 2 * HEAD_DIM] * cos + qf[:, b + 2 * HEAD_DIM:b + 3 * HEAD_DIM] * sin
        q_ref[:, h * C_QK + C_NOPE:(h + 1) * C_QK] = rope.astype(q_ref.dtype)
        k_ref[:, h * C_QK:h * C_QK + C_NOPE] = kvf[:, h * C_NOPE:(h + 1) * C_NOPE].astype(k_ref.dtype)
        k_ref[:, h * C_QK + C_NOPE:(h + 1) * C_QK] = kpe
    v_ref[...] = kvf[:, C_HEADS * C_NOPE:].astype(v_ref.dtype)


def _rope_tables(s_len):
    half = C_ROPE // 2
    inv = ROPE_THETA ** (-jnp.arange(half, dtype=F32) / half)
    ang = jnp.arange(s_len, dtype=F32)[:, None] * inv[None, :]
    cos = jnp.cos(ang)
    sin = jnp.sin(ang)
    zeros = jnp.zeros((s_len, HEAD_DIM - C_ROPE), F32)
    return (jnp.concatenate([cos, cos, zeros], axis=-1),
            jnp.concatenate([-sin, sin, zeros], axis=-1))


def _mla_proj(h2, s_len, gq, gkv, wq, wkv):
    n = h2.shape[0]
    tm = min(MLA_TM, s_len)
    cos, sin = _rope_tables(s_len)
    per_seq = s_len // tm
    qk_shape = jax.ShapeDtypeStruct((n, C_HEADS * C_QK), BF16)
    cq_col = (D_IN - C_ROPE - C_KV_RANK - C_Q_RANK) // C_Q_RANK
    ckv_col = (D_IN - C_ROPE - C_KV_RANK) // C_KV_RANK
    kr_col = KR_OFF // (2 * HEAD_DIM)
    return pl.pallas_call(
        _mla_proj_kernel,
        grid=(n // tm,),
        in_specs=[pl.BlockSpec((tm, C_Q_RANK), lambda i: (i, cq_col)),
                  pl.BlockSpec((tm, C_KV_RANK), lambda i: (i, ckv_col)),
                  pl.BlockSpec((tm, 2 * HEAD_DIM), lambda i: (i, kr_col)),
                  pl.BlockSpec((1, C_Q_RANK), lambda i: (0, 0)),
                  pl.BlockSpec((1, C_KV_RANK), lambda i: (0, 0)),
                  pl.BlockSpec(wq.shape, lambda i: (0, 0)),
                  pl.BlockSpec(wkv.shape, lambda i: (0, 0)),
                  pl.BlockSpec((tm, HEAD_DIM), lambda i: (i % per_seq, 0)),
                  pl.BlockSpec((tm, HEAD_DIM), lambda i: (i % per_seq, 0))],
        out_specs=[pl.BlockSpec((tm, C_HEADS * C_QK), lambda i: (i, 0)),
                   pl.BlockSpec((tm, C_HEADS * C_QK), lambda i: (i, 0)),
                   pl.BlockSpec((tm, C_OUT), lambda i: (i, 0))],
        out_shape=[qk_shape, qk_shape, jax.ShapeDtypeStruct((n, C_OUT), BF16)],
        compiler_params=_cparams(("parallel",)),
        name="mla_proj",
    )(h2, h2, h2, gq.reshape(1, -1), gkv.reshape(1, -1), wq, wkv, cos, sin)


def _mla_attn_kernel(q_ref, k_ref, v_ref, g_ref, o_ref):
    scale = (C_NOPE + C_ROPE) ** -0.5
    outs = []
    ssq = jnp.zeros((q_ref.shape[0], 1), F32)
    for h in range(C_HEADS):
        qk = slice(h * C_QK, (h + 1) * C_QK)
        s = lax.dot_general(q_ref[:, qk], k_ref[:, qk], (((1,), (1,)), ((), ())),
                            preferred_element_type=F32) * scale
        m = jnp.max(s, axis=-1, keepdims=True)
        p = jnp.exp(s - m)
        p = p / jnp.sum(p, axis=-1, keepdims=True)
        o = jnp.dot(p.astype(BF16), v_ref[:, h * C_V:(h + 1) * C_V],
                    preferred_element_type=F32)
        ssq = ssq + jnp.sum(o * o, axis=-1, keepdims=True)
        outs.append(o)
    inv = lax.rsqrt(ssq / C_OUT + EPS)
    for h in range(C_HEADS):
        cols = slice(h * C_V, (h + 1) * C_V)
        o_ref[:, cols] = (outs[h] * inv * g_ref[:, cols]).astype(o_ref.dtype)


def _mla_attn(q3, k3, v3, gain):
    bsz, s_len, _ = q3.shape
    tq = min(MLA_TQ, s_len)
    return pl.pallas_call(
        _mla_attn_kernel,
        grid=(bsz, s_len // tq),
        in_specs=[pl.BlockSpec((None, tq, C_HEADS * C_QK), lambda b, i: (b, i, 0)),
                  pl.BlockSpec((None, s_len, C_HEADS * C_QK), lambda b, i: (b, 0, 0)),
                  pl.BlockSpec((None, s_len, C_OUT), lambda b, i: (b, 0, 0)),
                  pl.BlockSpec((1, C_OUT), lambda b, i: (0, 0))],
        out_specs=pl.BlockSpec((None, tq, C_OUT), lambda b, i: (b, i, 0)),
        out_shape=jax.ShapeDtypeStruct((bsz, s_len, C_OUT), BF16),
        compiler_params=_cparams(("parallel", "arbitrary")),
        name="mla_attn",
    )(q3, k3, v3, gain.reshape(1, C_OUT))


def _out_proj_kernel(x_ref, oa_ref, ob_ref, oc_ref, wa_ref, wb_ref, wc_ref, o_ref):
    acc = jnp.dot(oa_ref[...], wa_ref[...], preferred_element_type=F32)
    acc = acc + jnp.dot(ob_ref[...], wb_ref[...], preferred_element_type=F32)
    acc = acc + jnp.dot(oc_ref[...], wc_ref[...], preferred_element_type=F32)
    o_ref[...] = x_ref[...] + acc


def _out_proj(x, oa, ob, oc, w_o, tm=1024, tn=512):
    n, d = x.shape
    tm = min(tm, n)
    return pl.pallas_call(
        _out_proj_kernel,
        grid=(n // tm, d // tn),
        in_specs=[pl.BlockSpec((tm, tn), lambda i, j: (i, j)),
                  pl.BlockSpec((tm, A_Q), lambda i, j: (i, 0)),
                  pl.BlockSpec((tm, B_W), lambda i, j: (i, 0)),
                  pl.BlockSpec((tm, C_OUT), lambda i, j: (i, 0)),
                  pl.BlockSpec((A_Q, tn), lambda i, j: (0, j)),
                  pl.BlockSpec((B_W, tn), lambda i, j: (A_Q // B_W, j)),
                  pl.BlockSpec((C_OUT, tn), lambda i, j: ((A_Q + B_W) // C_OUT, j))],
        out_specs=pl.BlockSpec((tm, tn), lambda i, j: (i, j)),
        out_shape=jax.ShapeDtypeStruct((n, d), F32),
        compiler_params=_cparams(("parallel", "arbitrary")),
        name="out_proj",
    )(x, oa, ob, oc, w_o, w_o, w_o)


ROUTE_TM = 256
_CAND_ROWS = [(a, PEER_TOPK // (a + 1)) for a in range(PEER_TOPK)]
SUBLANES = 8


def _top_ranks(s, idx):
    rank = jnp.full(s.shape, float(PEER_KEYS), F32)
    vals = []
    for k in range(PEER_TOPK):
        m = jnp.max(s, axis=0, keepdims=True)
        first = jnp.min(jnp.where(s == m, idx, float(PEER_KEYS)), axis=0, keepdims=True)
        hit = idx == first
        rank = jnp.where(hit, float(k), rank)
        s = jnp.where(hit, NEG_INF, s)
        vals.append(m)
    return rank, vals


def _bf16_bits_high(x):
    return pltpu.bitcast(x.astype(BF16).astype(F32), jnp.uint32)


def _pack_halves(x):
    half = x.shape[0] // 2
    return (_bf16_bits_high(x[:half]) >> 16) | _bf16_bits_high(x[half:])


def _pack_twice(x):
    bits = _bf16_bits_high(x)
    return (bits >> 16) | bits


def _peer_route_kernel(x_ref, g_ref, wq_ref, keys_ref, xnt_ref, e2_ref, r2_ref, n1_ref, w1_ref):
    tm = x_ref.shape[0]
    xn = _rms(x_ref[...], g_ref[...])
    xnt = xn.T.astype(BF16)
    xnt_ref[...] = xnt
    qt = jnp.dot(wq_ref[...], xnt, preferred_element_type=F32)
    row = lax.broadcasted_iota(jnp.int32, (PEER_KEYS, tm), 0).astype(F32)
    for h in range(PEER_HEADS):
        sc = []
        for c in range(2):
            row0 = (h * 2 + c) * PEER_HALF
            sc.append(jnp.dot(keys_ref[c], qt[row0:row0 + PEER_HALF, :],
                              preferred_element_type=F32,
                              precision=lax.Precision.HIGHEST))
        rank1, t1 = _top_ranks(sc[0], row)
        rank2, t2 = _top_ranks(sc[1], jnp.where(row < PEER_KEYS // 2, 2.0 * row, 2.0 * row - (PEER_KEYS - 1)))
        e1 = [jnp.exp(t1[a] - t1[0]) for a in range(PEER_TOPK)]
        e2 = [jnp.exp(t2[b] - t2[0]) for b in range(PEER_TOPK)]
        groups, gmeta = [], []
        for a, nb in _CAND_ROWS:
            for b0 in range(0, nb, SUBLANES):
                bs = list(range(b0, min(b0 + SUBLANES, nb)))
                rows = [t1[a] + t2[b] for b in bs]
                rows += [jnp.full((1, tm), NEG_INF, F32)] * (SUBLANES - len(bs))
                groups.append(jnp.concatenate(rows, axis=0))
                gmeta.append((a, bs))
        cand = jnp.concatenate(groups, axis=0)
        cidx = lax.broadcasted_iota(jnp.int32, cand.shape, 0).astype(F32)
        self_f = jnp.zeros(cand.shape, F32)
        for _ in range(PEER_TOPK):
            m = jnp.max(cand, axis=0, keepdims=True)
            first = jnp.min(jnp.where(cand == m, cidx, float(cand.shape[0])), axis=0, keepdims=True)
            hit = cidx == first
            self_f = jnp.where(hit, 1.0, self_f)
            cand = jnp.where(hit, NEG_INF, cand)
        n_a = [jnp.zeros((1, tm), F32) for _ in range(PEER_TOPK)]
        z = jnp.zeros((1, tm), F32)
        for gi, (a, bs) in enumerate(gmeta):
            blk = self_f[gi * SUBLANES:(gi + 1) * SUBLANES, :]
            n_a[a] = n_a[a] + jnp.sum(blk, axis=0, keepdims=True)
            for bi, b in enumerate(bs):
                z = z + blk[bi:bi + 1, :] * (e1[a] * e2[b])
        n1 = jnp.zeros(rank1.shape, F32)
        for a in range(PEER_TOPK):
            n1 = jnp.where(rank1 == float(a), n_a[a], n1)
        e2_ref[h] = _pack_halves(jnp.exp(sc[1] - t2[0]))
        r2_ref[h] = _pack_halves(rank2)
        n1_ref[h] = _pack_twice(n1).reshape(PEER_KEYS // SUBLANES, SUBLANES, tm)
        w1_ref[h] = _pack_twice(jnp.exp(sc[0] - t1[0]) / z).reshape(PEER_KEYS // SUBLANES, SUBLANES, tm)


def _peer_route(x, g, wq_t, sub_keys):
    n, d = x.shape
    tm = min(ROUTE_TM, n)
    plane = jax.ShapeDtypeStruct((PEER_HEADS, PEER_KEYS // 2, n), jnp.uint32)
    plane_spec = pl.BlockSpec((PEER_HEADS, PEER_KEYS // 2, tm), lambda i: (0, 0, i))
    table = jax.ShapeDtypeStruct((PEER_HEADS, PEER_KEYS // SUBLANES, SUBLANES, n), jnp.uint32)
    table_spec = pl.BlockSpec((PEER_HEADS, PEER_KEYS // SUBLANES, SUBLANES, tm),
                              lambda i: (0, 0, 0, i))
    return pl.pallas_call(
        _peer_route_kernel,
        grid=(n // tm,),
        in_specs=[pl.BlockSpec((tm, d), lambda i: (i, 0)),
                  pl.BlockSpec((1, d), lambda i: (0, 0)),
                  pl.BlockSpec(wq_t.shape, lambda i: (0, 0)),
                  pl.BlockSpec(sub_keys.shape, lambda i: (0, 0, 0))],
        out_specs=[pl.BlockSpec((d, tm), lambda i: (0, i)),
                   plane_spec, plane_spec, table_spec, table_spec],
        out_shape=[jax.ShapeDtypeStruct((d, n), BF16), plane, plane, table, table],
        compiler_params=_cparams(("parallel",)),
        name="peer_route",
    )(x, g.reshape(1, d), wq_t, sub_keys)


PEER_T = 512
PEER_ET = SUBLANES * PEER_KEYS
LANES = 128


def _rows_bf16(words):
    return pltpu.bitcast(jnp.broadcast_to(words, (PEER_KEYS // 2, LANES)), BF16)


def _peer_dense_kernel(u_ref, vt_ref, xnt_ref, e2_ref, r2_ref, n1_ref, w1_ref, yt_ref, h_ref):
    e = pl.program_id(1)
    t = xnt_ref.shape[1]

    @pl.when(e == 0)
    def _():
        yt_ref[...] = jnp.zeros_like(yt_ref)

    at = jnp.dot(u_ref[...], xnt_ref[...], preferred_element_type=F32)
    for rl in range(SUBLANES):
        for tc in range(t // LANES):
            cols = slice(tc * LANES, (tc + 1) * LANES)
            gate = jnp.zeros((PEER_KEYS, LANES), BF16)
            for h in range(PEER_HEADS):
                n1r = _rows_bf16(n1_ref[h, e, rl:rl + 1, cols])
                w1r = _rows_bf16(w1_ref[h, e, rl:rl + 1, cols])
                r2 = pltpu.bitcast(r2_ref[h, :, cols], BF16)
                e2 = pltpu.bitcast(e2_ref[h, :, cols], BF16)
                gate = gate + jnp.where(r2 < n1r, e2 * w1r, jnp.zeros((), BF16))
            a = at[rl * PEER_KEYS:(rl + 1) * PEER_KEYS, cols]
            h_ref[rl * PEER_KEYS:(rl + 1) * PEER_KEYS, cols] = jax.nn.gelu(a.astype(BF16)) * gate
    yt_ref[...] += jnp.dot(vt_ref[...], h_ref[...], preferred_element_type=F32)


def _peer_dense(u, vt, xnt, e2, r2, n1, w1):
    n_exp, d = u.shape
    n = xnt.shape[1]
    t = min(PEER_T, n)
    plane_spec = pl.BlockSpec((PEER_HEADS, PEER_KEYS // 2, t), lambda i, e: (0, 0, i))
    table_spec = pl.BlockSpec((PEER_HEADS, PEER_KEYS // SUBLANES, SUBLANES, t),
                              lambda i, e: (0, 0, 0, i))
    return pl.pallas_call(
        _peer_dense_kernel,
        grid=(n // t, n_exp // PEER_ET),
        in_specs=[pl.BlockSpec((PEER_ET, d), lambda i, e: (e, 0)),
                  pl.BlockSpec((d, PEER_ET), lambda i, e: (0, e)),
                  pl.BlockSpec((d, t), lambda i, e: (0, i)),
                  plane_spec, plane_spec, table_spec, table_spec],
        out_specs=pl.BlockSpec((d, t), lambda i, e: (0, i)),
        out_shape=jax.ShapeDtypeStruct((d, n), F32),
        scratch_shapes=[pltpu.VMEM((PEER_ET, t), BF16)],
        compiler_params=_cparams(("parallel", "arbitrary")),
        name="peer_dense",
    )(u, vt, xnt, e2, r2, n1, w1)


def _add_t_kernel(x_ref, yt_ref, o_ref):
    o_ref[...] = x_ref[...] + yt_ref[...].T


def _add_t_norm_kernel(x_ref, yt_ref, g_ref, o_ref):
    o_ref[...] = _rms(x_ref[...] + yt_ref[...].T, g_ref[...])


def _add_transposed(x, yt, gain=None, tm=512):
    n, d = x.shape
    tm = min(tm, n)
    in_specs = [pl.BlockSpec((tm, d), lambda i: (i, 0)),
                pl.BlockSpec((d, tm), lambda i: (0, i))]
    args = [x, yt]
    body = _add_t_kernel
    if gain is not None:
        in_specs.append(pl.BlockSpec((1, d), lambda i: (0, 0)))
        args.append(gain.reshape(1, d))
        body = _add_t_norm_kernel
    return pl.pallas_call(
        body,
        grid=(n // tm,),
        in_specs=in_specs,
        out_specs=pl.BlockSpec((tm, d), lambda i: (i, 0)),
        out_shape=jax.ShapeDtypeStruct((n, d), F32),
        compiler_params=_cparams(("parallel",)),
        name="add_transposed",
    )(*args)


def _swap_halves(w):
    half = w.shape[-1] // 2
    return jnp.concatenate([w[..., half:], w[..., :half]], axis=-1)


def _prep_w_in(w_in):
    d = w_in.shape[0]
    kr = w_in[:, KR_OFF:D_IN]
    z = jnp.zeros((d, HEAD_DIM - C_ROPE), w_in.dtype)
    return jnp.concatenate([w_in[:, :KR_OFF], kr, z, _swap_halves(kr), z], axis=-1).astype(BF16)


def _prep_w_uq(w_uq):
    r = w_uq.shape[0]
    z = jnp.zeros((r, HEAD_DIM - C_ROPE), w_uq.dtype)
    blocks = []
    for h in range(C_HEADS):
        b = h * (C_NOPE + C_ROPE)
        rope = w_uq[:, b + C_NOPE:b + C_NOPE + C_ROPE]
        blocks += [w_uq[:, b:b + C_NOPE], rope, z, _swap_halves(rope), z]
    return jnp.concatenate(blocks, axis=-1).astype(BF16)


def _prep_w_ukv(w_ukv):
    ks = [w_ukv[:, h * (C_NOPE + C_V):h * (C_NOPE + C_V) + C_NOPE] for h in range(C_HEADS)]
    vs = [w_ukv[:, h * (C_NOPE + C_V) + C_NOPE:(h + 1) * (C_NOPE + C_V)] for h in range(C_HEADS)]
    return jnp.concatenate(ks + vs, axis=-1).astype(BF16)


def _mixer(x2, bsz, s_len, ln1, w_in, a_sink, b_rel_bias, c_q_norm, c_kv_norm, c_w_uq, c_w_ukv,
           out_norm, w_o):
    h2 = _norm_matmul(x2, ln1, _prep_w_in(w_in))
    h3 = h2.reshape(bsz, s_len, D_IN_PAD)
    oa = _attn_a(h3, a_sink.astype(F32), out_norm[:A_Q])
    ob = _attn_b(h3, b_rel_bias, out_norm[A_Q:A_Q + B_W])
    qc, kc, vc = _mla_proj(h2, s_len, c_q_norm, c_kv_norm, _prep_w_uq(c_w_uq), _prep_w_ukv(c_w_ukv))
    oc = _mla_attn(qc.reshape(bsz, s_len, -1), kc.reshape(bsz, s_len, -1),
                   vc.reshape(bsz, s_len, -1), out_norm[A_Q + B_W:])
    n = bsz * s_len
    return _out_proj(x2, oa.reshape(n, A_Q), ob.reshape(n, B_W), oc.reshape(n, C_OUT),
                     w_o.astype(BF16))


def _peer(x2, ln2, w_q, sub_keys, u, vv):
    keys = sub_keys.astype(F32)
    keys = jnp.stack([keys[0], jnp.concatenate([keys[1, 0::2], keys[1, 1::2]], axis=0)])
    xnt, e2, r2, n1, w1 = _peer_route(x2, ln2, w_q.T.astype(BF16), keys)
    return _peer_dense(u.astype(BF16), vv.T.astype(BF16), xnt, e2, r2, n1, w1)


def kernel(x, ln1, w_in, a_sink, b_rel_bias, c_q_norm, c_kv_norm, c_w_uq, c_w_ukv, out_norm, w_o,
           ln2, peer_w_q, peer_sub_keys, peer_u, peer_v, final_norm):
    bsz, s_len, d = x.shape
    depth = ln1.shape[0]
    x2 = x.reshape(bsz * s_len, d)
    for l in range(depth):
        x2 = _mixer(x2, bsz, s_len, ln1[l], w_in[l], a_sink[l], b_rel_bias[l], c_q_norm[l],
                    c_kv_norm[l], c_w_uq[l], c_w_ukv[l], out_norm[l], w_o[l])
        yt = _peer(x2, ln2[l], peer_w_q[l], peer_sub_keys[l], peer_u[l], peer_v[l])
        x2 = _add_transposed(x2, yt, final_norm if l == depth - 1 else None)
    return x2.reshape(bsz, s_len, d)
```

```python
import functools
import math

import jax
import jax.numpy as jnp
import numpy as np
from jax import lax
from jax.experimental import pallas as pl
from jax.experimental.pallas import tpu as pltpu

F32 = jnp.float32
BF16 = jnp.bfloat16

EPS = 1e-6
HEAD_DIM = 128
A_HEADS = 8
A_KV_HEADS = 2
A_GROUP = A_HEADS // A_KV_HEADS
A_WINDOW = 128
B_HEADS = 4
GRID_W = 64
NB_ROWS = 8
NB_COLS = 16
C_HEADS = 4
C_Q_RANK = 512
C_KV_RANK = 256
C_NOPE = 128
C_ROPE = 64
C_V = 128
ROPE_THETA = 10000.0
A_Q = A_HEADS * HEAD_DIM
A_KV = A_KV_HEADS * HEAD_DIM
B_W = B_HEADS * HEAD_DIM
C_OUT = C_HEADS * C_V
D_MIX = A_Q + B_W + C_OUT
D_IN = A_Q + 2 * A_KV + 3 * B_W + C_Q_RANK + C_KV_RANK + C_ROPE
D_IN_PAD = 4096
KR_OFF = D_IN - C_ROPE
PEER_HEADS = 8
PEER_KEYS = 128
PEER_HALF = 64
PEER_TOPK = 16

VMEM_LIMIT = 56 * 1024 * 1024
NEG_INF = float("-inf")


def _cparams(sem):
    return pltpu.CompilerParams(dimension_semantics=sem, vmem_limit_bytes=VMEM_LIMIT)


def _rms(x, g):
    ms = jnp.mean(x * x, axis=-1, keepdims=True)
    return x * lax.rsqrt(ms + EPS) * g


def _norm_matmul_kernel(x_ref, g_ref, w_ref, o_ref, xn_ref):
    @pl.when(pl.program_id(1) == 0)
    def _():
        xn_ref[...] = _rms(x_ref[...], g_ref[...]).astype(BF16)

    o_ref[...] = jnp.dot(xn_ref[...], w_ref[...],
                         preferred_element_type=F32).astype(o_ref.dtype)


def _norm_matmul(x, g, w, tm=1024, tn=512):
    n, d = x.shape
    nout = w.shape[1]
    tm = min(tm, n)
    return pl.pallas_call(
        _norm_matmul_kernel,
        grid=(n // tm, nout // tn),
        in_specs=[pl.BlockSpec((tm, d), lambda i, j: (i, 0)),
                  pl.BlockSpec((1, d), lambda i, j: (0, 0)),
                  pl.BlockSpec((d, tn), lambda i, j: (0, j))],
        out_specs=pl.BlockSpec((tm, tn), lambda i, j: (i, j)),
        out_shape=jax.ShapeDtypeStruct((n, nout), BF16),
        scratch_shapes=[pltpu.VMEM((tm, d), BF16)],
        compiler_params=_cparams(("parallel", "arbitrary")),
        name="norm_matmul",
    )(x, g.reshape(1, d), w)


A_TQ = 512
A_BAND = 3 * A_WINDOW


def _attn_a_kernel(sink_ref, q_ref, k_ref, v_ref, g_ref, o_ref):
    i = pl.program_id(1)
    s_len = k_ref.shape[0]
    scale = HEAD_DIM ** -0.5
    slopes = [2.0 ** (-8.0 * (h + 1) / A_HEADS) for h in range(A_HEADS)]
    for j in range(A_TQ // A_WINDOW):
        q0 = i * A_TQ + j * A_WINDOW
        start = pl.multiple_of(jnp.clip(q0 - A_WINDOW, 0, s_len - A_BAND), A_WINDOW)
        kb = k_ref[pl.ds(start, A_BAND), :]
        vb = v_ref[pl.ds(start, A_BAND), :]
        q_pos = q0 + lax.broadcasted_iota(jnp.int32, (A_WINDOW, A_BAND), 0)
        k_pos = start + lax.broadcasted_iota(jnp.int32, (A_WINDOW, A_BAND), 1)
        dist = jnp.abs(q_pos - k_pos)
        valid = dist <= A_WINDOW
        dist_f = dist.astype(F32)
        outs = []
        ssq = jnp.zeros((A_WINDOW, 1), F32)
        for kh in range(A_KV_HEADS):
            kk = kb[:, kh * HEAD_DIM:(kh + 1) * HEAD_DIM]
            vv = vb[:, kh * HEAD_DIM:(kh + 1) * HEAD_DIM]
            for g in range(A_GROUP):
                h = kh * A_GROUP + g
                qh = q_ref[j * A_WINDOW:(j + 1) * A_WINDOW, h * HEAD_DIM:(h + 1) * HEAD_DIM]
                s = lax.dot_general(qh, kk, (((1,), (1,)), ((), ())),
                                    preferred_element_type=F32) * scale
                s = s - slopes[h] * dist_f
                s = jnp.where(valid, s, NEG_INF)
                sink = sink_ref[h]
                m = jnp.maximum(jnp.max(s, axis=-1, keepdims=True), sink)
                p = jnp.exp(s - m)
                denom = jnp.sum(p, axis=-1, keepdims=True) + jnp.exp(sink - m)
                p = p / denom
                o = jnp.dot(p.astype(BF16), vv, preferred_element_type=F32)
                ssq = ssq + jnp.sum(o * o, axis=-1, keepdims=True)
                outs.append(o)
        inv = lax.rsqrt(ssq / A_Q + EPS)
        for h in range(A_HEADS):
            cols = slice(h * HEAD_DIM, (h + 1) * HEAD_DIM)
            o_ref[j * A_WINDOW:(j + 1) * A_WINDOW, cols] = (
                outs[h] * inv * g_ref[:, cols]).astype(o_ref.dtype)


def _attn_a(h3, sink, gain):
    bsz, s_len, _ = h3.shape
    assert s_len % A_TQ == 0 and s_len >= A_BAND
    kcol = A_Q // A_KV
    return pl.pallas_call(
        _attn_a_kernel,
        grid=(bsz, s_len // A_TQ),
        in_specs=[pl.BlockSpec(memory_space=pltpu.SMEM),
                  pl.BlockSpec((None, A_TQ, A_Q), lambda b, i: (b, i, 0)),
                  pl.BlockSpec((None, s_len, A_KV), lambda b, i: (b, 0, kcol)),
                  pl.BlockSpec((None, s_len, A_KV), lambda b, i: (b, 0, kcol + 1)),
                  pl.BlockSpec((1, A_Q), lambda b, i: (0, 0))],
        out_specs=pl.BlockSpec((None, A_TQ, A_Q), lambda b, i: (b, i, 0)),
        out_shape=jax.ShapeDtypeStruct((bsz, s_len, A_Q), BF16),
        compiler_params=_cparams(("parallel", "arbitrary")),
        name="attn_a",
    )(sink, h3, h3, h3, gain.reshape(1, A_Q))


B_RQ = 8
B_KEYS = NB_ROWS * GRID_W
B_QCOL = (A_Q + 2 * A_KV) // B_W


def _attn_b_kernel(q_ref, k_ref, v_ref, bias_ref, g_ref, o_ref):
    i = pl.program_id(1)
    rows = k_ref.shape[0] // GRID_W
    scale = HEAD_DIM ** -0.5
    for jr in range(B_RQ):
        r = i * B_RQ + jr
        r_start = jnp.clip(r - NB_ROWS // 2, 0, rows - NB_ROWS)
        off = r_start - r + NB_ROWS - 1
        kstart = pl.multiple_of(r_start * GRID_W, GRID_W)
        kb = k_ref[pl.ds(kstart, B_KEYS), :]
        vb = v_ref[pl.ds(kstart, B_KEYS), :]
        hs = [slice(h * HEAD_DIM, (h + 1) * HEAD_DIM) for h in range(B_HEADS)]
        q3 = jnp.stack([q_ref[jr * GRID_W:(jr + 1) * GRID_W, c] for c in hs])
        k3 = jnp.stack([kb[:, c] for c in hs])
        v3 = jnp.stack([vb[:, c] for c in hs])
        s = jnp.einsum('hqd,hkd->hqk', q3, k3, preferred_element_type=F32) * scale + bias_ref[off]
        m = jnp.max(s, axis=-1, keepdims=True)
        p = jnp.exp(s - m)
        p = p / jnp.sum(p, axis=-1, keepdims=True)
        o = jnp.einsum('hqk,hkd->hqd', p.astype(BF16), v3, preferred_element_type=F32)
        ssq = jnp.sum(jnp.sum(o * o, axis=-1, keepdims=True), axis=0)
        outs = [o[h] for h in range(B_HEADS)]
        inv = lax.rsqrt(ssq / B_W + EPS)
        for h in range(B_HEADS):
            cols = slice(h * HEAD_DIM, (h + 1) * HEAD_DIM)
            o_ref[jr * GRID_W:(jr + 1) * GRID_W, cols] = (
                outs[h] * inv * g_ref[:, cols]).astype(o_ref.dtype)


def _nb_bias_table(rel_bias):
    col = jnp.arange(GRID_W)
    c_start = jnp.clip(col - NB_COLS // 2, 0, GRID_W - NB_COLS)
    col_ok = (col[None, :] >= c_start[:, None]) & (col[None, :] < c_start[:, None] + NB_COLS)
    dc_idx = jnp.clip(col[None, :] - col[:, None] + NB_COLS - 1, 0, 2 * NB_COLS - 2)
    rb = rel_bias.astype(F32)[:, :, dc_idx]
    rb = jnp.where(col_ok[None, None], rb, NEG_INF)
    tabs = []
    for off in range(NB_ROWS):
        t = jnp.transpose(rb[:, off:off + NB_ROWS], (0, 2, 1, 3))
        tabs.append(t.reshape(B_HEADS, GRID_W, B_KEYS))
    return jnp.stack(tabs)


def _attn_b(h3, rel_bias, gain):
    bsz, s_len, _ = h3.shape
    rows = s_len // GRID_W
    assert rows >= NB_ROWS and rows % B_RQ == 0
    tq = B_RQ * GRID_W
    bias = _nb_bias_table(rel_bias)
    return pl.pallas_call(
        _attn_b_kernel,
        grid=(bsz, rows // B_RQ),
        in_specs=[pl.BlockSpec((None, tq, B_W), lambda b, i: (b, i, B_QCOL)),
                  pl.BlockSpec((None, s_len, B_W), lambda b, i: (b, 0, B_QCOL + 1)),
                  pl.BlockSpec((None, s_len, B_W), lambda b, i: (b, 0, B_QCOL + 2)),
                  pl.BlockSpec((NB_ROWS, B_HEADS, GRID_W, B_KEYS), lambda b, i: (0, 0, 0, 0)),
                  pl.BlockSpec((1, B_W), lambda b, i: (0, 0))],
        out_specs=pl.BlockSpec((None, tq, B_W), lambda b, i: (b, i, 0)),
        out_shape=jax.ShapeDtypeStruct((bsz, s_len, B_W), BF16),
        compiler_params=_cparams(("parallel", "arbitrary")),
        name="attn_b",
    )(h3, h3, h3, bias, gain.reshape(1, B_W))


C_QK = 256
C_QW = 3 * HEAD_DIM
MLA_TM = 512
MLA_TQ = 256


def _mla_proj_kernel(cq_ref, ckv_ref, kr_ref, gq_ref, gkv_ref, wq_ref, wkv_ref,
                     cos_ref, sin_ref, q_ref, k_ref, v_ref):
    cos = cos_ref[...]
    sin = sin_ref[...]
    cqn = _rms(cq_ref[...].astype(F32), gq_ref[...]).astype(BF16)
    qf = jnp.dot(cqn, wq_ref[...], preferred_element_type=F32)
    ckvn = _rms(ckv_ref[...].astype(F32), gkv_ref[...]).astype(BF16)
    kvf = jnp.dot(ckvn, wkv_ref[...], preferred_element_type=F32)
    kr = kr_ref[...].astype(F32)
    kpe = (kr[:, :HEAD_DIM] * cos + kr[:, HEAD_DIM:] * sin).astype(k_ref.dtype)
    for h in range(C_HEADS):
        b = h * C_QW
        q_ref[:, h * C_QK:h * C_QK + C_NOPE] = qf[:, b:b + C_NOPE].astype(q_ref.dtype)
        rope = qf[:, b + HEAD_DIM:b + 2 * HEAD_DIM] * cos + qf[:, b + 2 * HEAD_DIM:b + 3 * HEAD_DIM] * sin
        q_ref[:, h * C_QK + C_NOPE:(h + 1) * C_QK] = rope.astype(q_ref.dtype)
        k_ref[:, h * C_QK:h * C_QK + C_NOPE] = kvf[:, h * C_NOPE:(h + 1) * C_NOPE].astype(k_ref.dtype)
        k_ref[:, h * C_QK + C_NOPE:(h + 1) * C_QK] = kpe
    v_ref[...] = kvf[:, C_HEADS * C_NOPE:].astype(v_ref.dtype)


def _rope_tables(s_len):
    half = C_ROPE // 2
    inv = ROPE_THETA ** (-jnp.arange(half, dtype=F32) / half)
    ang = jnp.arange(s_len, dtype=F32)[:, None] * inv[None, :]
    cos = jnp.cos(ang)
    sin = jnp.sin(ang)
    zeros = jnp.zeros((s_len, HEAD_DIM - C_ROPE), F32)
    return (jnp.concatenate([cos, cos, zeros], axis=-1),
            jnp.concatenate([-sin, sin, zeros], axis=-1))


def _mla_proj(h2, s_len, gq, gkv, wq, wkv):
    n = h2.shape[0]
    tm = min(MLA_TM, s_len)
    cos, sin = _rope_tables(s_len)
    per_seq = s_len // tm
    qk_shape = jax.ShapeDtypeStruct((n, C_HEADS * C_QK), BF16)
    cq_col = (D_IN - C_ROPE - C_KV_RANK - C_Q_RANK) // C_Q_RANK
    ckv_col = (D_IN - C_ROPE - C_KV_RANK) // C_KV_RANK
    kr_col = KR_OFF // (2 * HEAD_DIM)
    return pl.pallas_call(
        _mla_proj_kernel,
        grid=(n // tm,),
        in_specs=[pl.BlockSpec((tm, C_Q_RANK), lambda i: (i, cq_col)),
                  pl.BlockSpec((tm, C_KV_RANK), lambda i: (i, ckv_col)),
                  pl.BlockSpec((tm, 2 * HEAD_DIM), lambda i: (i, kr_col)),
                  pl.BlockSpec((1, C_Q_RANK), lambda i: (0, 0)),
                  pl.BlockSpec((1, C_KV_RANK), lambda i: (0, 0)),
                  pl.BlockSpec(wq.shape, lambda i: (0, 0)),
                  pl.BlockSpec(wkv.shape, lambda i: (0, 0)),
                  pl.BlockSpec((tm, HEAD_DIM), lambda i: (i % per_seq, 0)),
                  pl.BlockSpec((tm, HEAD_DIM), lambda i: (i % per_seq, 0))],
        out_specs=[pl.BlockSpec((tm, C_HEADS * C_QK), lambda i: (i, 0)),
                   pl.BlockSpec((tm, C_HEADS * C_QK), lambda i: (i, 0)),
                   pl.BlockSpec((tm, C_OUT), lambda i: (i, 0))],
        out_shape=[qk_shape, qk_shape, jax.ShapeDtypeStruct((n, C_OUT), BF16)],
        compiler_params=_cparams(("parallel",)),
        name="mla_proj",
    )(h2, h2, h2, gq.reshape(1, -1), gkv.reshape(1, -1), wq, wkv, cos, sin)


def _mla_attn_kernel(q_ref, k_ref, v_ref, g_ref, o_ref):
    scale = (C_NOPE + C_ROPE) ** -0.5
    outs = []
    ssq = jnp.zeros((q_ref.shape[0], 1), F32)
    for h in range(C_HEADS):
        qk = slice(h * C_QK, (h + 1) * C_QK)
        s = lax.dot_general(q_ref[:, qk], k_ref[:, qk], (((1,), (1,)), ((), ())),
                            preferred_element_type=F32) * scale
        m = jnp.max(s, axis=-1, keepdims=True)
        p = jnp.exp(s - m)
        p = p / jnp.sum(p, axis=-1, keepdims=True)
        o = jnp.dot(p.astype(BF16), v_ref[:, h * C_V:(h + 1) * C_V],
                    preferred_element_type=F32)
        ssq = ssq + jnp.sum(o * o, axis=-1, keepdims=True)
        outs.append(o)
    inv = lax.rsqrt(ssq / C_OUT + EPS)
    for h in range(C_HEADS):
        cols = slice(h * C_V, (h + 1) * C_V)
        o_ref[:, cols] = (outs[h] * inv * g_ref[:, cols]).astype(o_ref.dtype)


def _mla_attn(q3, k3, v3, gain):
    bsz, s_len, _ = q3.shape
    tq = min(MLA_TQ, s_len)
    return pl.pallas_call(
        _mla_attn_kernel,
        grid=(bsz, s_len // tq),
        in_specs=[pl.BlockSpec((None, tq, C_HEADS * C_QK), lambda b, i: (b, i, 0)),
                  pl.BlockSpec((None, s_len, C_HEADS * C_QK), lambda b, i: (b, 0, 0)),
                  pl.BlockSpec((None, s_len, C_OUT), lambda b, i: (b, 0, 0)),
                  pl.BlockSpec((1, C_OUT), lambda b, i: (0, 0))],
        out_specs=pl.BlockSpec((None, tq, C_OUT), lambda b, i: (b, i, 0)),
        out_shape=jax.ShapeDtypeStruct((bsz, s_len, C_OUT), BF16),
        compiler_params=_cparams(("parallel", "arbitrary")),
        name="mla_attn",
    )(q3, k3, v3, gain.reshape(1, C_OUT))


def _out_proj_kernel(x_ref, oa_ref, ob_ref, oc_ref, wa_ref, wb_ref, wc_ref, o_ref):
    acc = jnp.dot(oa_ref[...], wa_ref[...], preferred_element_type=F32)
    acc = acc + jnp.dot(ob_ref[...], wb_ref[...], preferred_element_type=F32)
    acc = acc + jnp.dot(oc_ref[...], wc_ref[...], preferred_element_type=F32)
    o_ref[...] = x_ref[...] + acc


def _out_proj(x, oa, ob, oc, w_o, tm=1024, tn=512):
    n, d = x.shape
    tm = min(tm, n)
    return pl.pallas_call(
        _out_proj_kernel,
        grid=(n // tm, d // tn),
        in_specs=[pl.BlockSpec((tm, tn), lambda i, j: (i, j)),
                  pl.BlockSpec((tm, A_Q), lambda i, j: (i, 0)),
                  pl.BlockSpec((tm, B_W), lambda i, j: (i, 0)),
                  pl.BlockSpec((tm, C_OUT), lambda i, j: (i, 0)),
                  pl.BlockSpec((A_Q, tn), lambda i, j: (0, j)),
                  pl.BlockSpec((B_W, tn), lambda i, j: (A_Q // B_W, j)),
                  pl.BlockSpec((C_OUT, tn), lambda i, j: ((A_Q + B_W) // C_OUT, j))],
        out_specs=pl.BlockSpec((tm, tn), lambda i, j: (i, j)),
        out_shape=jax.ShapeDtypeStruct((n, d), F32),
        compiler_params=_cparams(("parallel", "arbitrary")),
        name="out_proj",
    )(x, oa, ob, oc, w_o, w_o, w_o)


ROUTE_TM = 256
_CAND_ROWS = [(a, PEER_TOPK // (a + 1)) for a in range(PEER_TOPK)]
SUBLANES = 8


def _top_ranks(s, idx, exact):
    rank = jnp.full(s.shape, float(PEER_KEYS), F32)
    vals = []
    for k in range(PEER_TOPK):
        m = jnp.max(s, axis=0, keepdims=True)
        hit = s == m
        if exact:
            first = jnp.min(jnp.where(hit, idx, float(PEER_KEYS)), axis=0, keepdims=True)
            hit = idx == first
        rank = jnp.where(hit, float(k), rank)
        s = jnp.where(hit, NEG_INF, s)
        vals.append(m)
    return rank, vals


def _ranked_count(rank):
    return jnp.sum(jnp.where(rank < float(PEER_TOPK), 1.0, 0.0), axis=0, keepdims=True)


def _bf16_bits_high(x):
    return pltpu.bitcast(x.astype(BF16).astype(F32), jnp.uint32)


def _pack_halves(x):
    half = x.shape[0] // 2
    return (_bf16_bits_high(x[:half]) >> 16) | _bf16_bits_high(x[half:])


def _pack_twice(x):
    bits = _bf16_bits_high(x)
    return (bits >> 16) | bits


def _peer_route_kernel(x_ref, g_ref, wq_ref, keys_ref, xnt_ref, e2_ref, r2_ref, n1_ref, w1_ref):
    tm = x_ref.shape[0]
    xn = _rms(x_ref[...], g_ref[...])
    xnt = xn.T.astype(BF16)
    xnt_ref[...] = xnt
    qt = jnp.dot(wq_ref[...], xnt, preferred_element_type=F32)
    row = lax.broadcasted_iota(jnp.int32, (PEER_KEYS, tm), 0).astype(F32)
    row2 = jnp.where(row < PEER_KEYS // 2, 2.0 * row, 2.0 * row - (PEER_KEYS - 1))

    def route_head(h, exact):
        sc = []
        for c in range(2):
            row0 = (h * 2 + c) * PEER_HALF
            sc.append(jnp.dot(keys_ref[c], qt[row0:row0 + PEER_HALF, :],
                              preferred_element_type=F32,
                              precision=lax.Precision.HIGHEST))
        rank1, t1 = _top_ranks(sc[0], row, exact)
        rank2, t2 = _top_ranks(sc[1], row2, exact)
        e1 = [jnp.exp(t1[a] - t1[0]) for a in range(PEER_TOPK)]
        e2 = [jnp.exp(t2[b] - t2[0]) for b in range(PEER_TOPK)]
        groups, gmeta = [], []
        for a, nb in _CAND_ROWS:
            for b0 in range(0, nb, SUBLANES):
                bs = list(range(b0, min(b0 + SUBLANES, nb)))
                rows = [t1[a] + t2[b] for b in bs]
                rows += [jnp.full((1, tm), NEG_INF, F32)] * (SUBLANES - len(bs))
                groups.append(jnp.concatenate(rows, axis=0))
                gmeta.append((a, bs))
        cand = jnp.concatenate(groups, axis=0)
        cidx = lax.broadcasted_iota(jnp.int32, cand.shape, 0).astype(F32)
        crank, _ = _top_ranks(cand, cidx, exact)
        self_f = jnp.where(crank < float(PEER_TOPK), 1.0, 0.0)
        n_a = [jnp.zeros((1, tm), F32) for _ in range(PEER_TOPK)]
        z = jnp.zeros((1, tm), F32)
        for gi, (a, bs) in enumerate(gmeta):
            blk = self_f[gi * SUBLANES:(gi + 1) * SUBLANES, :]
            n_a[a] = n_a[a] + jnp.sum(blk, axis=0, keepdims=True)
            for bi, b in enumerate(bs):
                z = z + blk[bi:bi + 1, :] * (e1[a] * e2[b])
        n1 = jnp.zeros(rank1.shape, F32)
        for a in range(PEER_TOPK):
            n1 = jnp.where(rank1 == float(a), n_a[a], n1)
        e2_ref[h] = _pack_halves(jnp.exp(sc[1] - t2[0]))
        r2_ref[h] = _pack_halves(rank2)
        n1_ref[h] = _pack_twice(n1).reshape(PEER_KEYS // SUBLANES, SUBLANES, tm)
        w1_ref[h] = _pack_twice(jnp.exp(sc[0] - t1[0]) / z).reshape(PEER_KEYS // SUBLANES, SUBLANES, tm)
        if exact:
            return jnp.zeros((1, tm), F32)
        counts = [_ranked_count(rank1), _ranked_count(rank2), jnp.sum(self_f, axis=0, keepdims=True)]
        return sum(jnp.where(c == float(PEER_TOPK), 0.0, 1.0) for c in counts)

    tied = jnp.zeros((1, tm), F32)
    for h in range(PEER_HEADS):
        tied = tied + route_head(h, exact=False)

    @pl.when(jnp.max(tied) > 0.0)
    def _():
        for h in range(PEER_HEADS):
            route_head(h, exact=True)


def _peer_route(x, g, wq_t, sub_keys):
    n, d = x.shape
    tm = min(ROUTE_TM, n)
    plane = jax.ShapeDtypeStruct((PEER_HEADS, PEER_KEYS // 2, n), jnp.uint32)
    plane_spec = pl.BlockSpec((PEER_HEADS, PEER_KEYS // 2, tm), lambda i: (0, 0, i))
    table = jax.ShapeDtypeStruct((PEER_HEADS, PEER_KEYS // SUBLANES, SUBLANES, n), jnp.uint32)
    table_spec = pl.BlockSpec((PEER_HEADS, PEER_KEYS // SUBLANES, SUBLANES, tm),
                              lambda i: (0, 0, 0, i))
    return pl.pallas_call(
        _peer_route_kernel,
        grid=(n // tm,),
        in_specs=[pl.BlockSpec((tm, d), lambda i: (i, 0)),
                  pl.BlockSpec((1, d), lambda i: (0, 0)),
                  pl.BlockSpec(wq_t.shape, lambda i: (0, 0)),
                  pl.BlockSpec(sub_keys.shape, lambda i: (0, 0, 0))],
        out_specs=[pl.BlockSpec((d, tm), lambda i: (0, i)),
                   plane_spec, plane_spec, table_spec, table_spec],
        out_shape=[jax.ShapeDtypeStruct((d, n), BF16), plane, plane, table, table],
        compiler_params=_cparams(("parallel",)),
        name="peer_route",
    )(x, g.reshape(1, d), wq_t, sub_keys)


PEER_T = 512
PEER_ET = SUBLANES * PEER_KEYS
LANES = 128


def _rows_bf16(words):
    return pltpu.bitcast(jnp.broadcast_to(words, (PEER_KEYS // 2, LANES)), BF16)


def _peer_dense_kernel(u_ref, vt_ref, xnt_ref, e2_ref, r2_ref, n1_ref, w1_ref, yt_ref, h_ref):
    e = pl.program_id(1)
    t = xnt_ref.shape[1]

    @pl.when(e == 0)
    def _():
        yt_ref[...] = jnp.zeros_like(yt_ref)

    at = jnp.dot(u_ref[...], xnt_ref[...], preferred_element_type=F32)
    for rl in range(SUBLANES):
        for tc in range(t // LANES):
            cols = slice(tc * LANES, (tc + 1) * LANES)
            gate = jnp.zeros((PEER_KEYS, LANES), BF16)
            for h in range(PEER_HEADS):
                n1r = _rows_bf16(n1_ref[h, e, rl:rl + 1, cols])
                w1r = _rows_bf16(w1_ref[h, e, rl:rl + 1, cols])
                r2 = pltpu.bitcast(r2_ref[h, :, cols], BF16)
                e2 = pltpu.bitcast(e2_ref[h, :, cols], BF16)
                gate = gate + jnp.where(r2 < n1r, e2 * w1r, jnp.zeros((), BF16))
            a = at[rl * PEER_KEYS:(rl + 1) * PEER_KEYS, cols]
            h_ref[rl * PEER_KEYS:(rl + 1) * PEER_KEYS, cols] = jax.nn.gelu(a.astype(BF16)) * gate
    yt_ref[...] += jnp.dot(vt_ref[...], h_ref[...], preferred_element_type=F32)


def _peer_dense(u, vt, xnt, e2, r2, n1, w1):
    n_exp, d = u.shape
    n = xnt.shape[1]
    t = min(PEER_T, n)
    plane_spec = pl.BlockSpec((PEER_HEADS, PEER_KEYS // 2, t), lambda i, e: (0, 0, i))
    table_spec = pl.BlockSpec((PEER_HEADS, PEER_KEYS // SUBLANES, SUBLANES, t),
                              lambda i, e: (0, 0, 0, i))
    return pl.pallas_call(
        _peer_dense_kernel,
        grid=(n // t, n_exp // PEER_ET),
        in_specs=[pl.BlockSpec((PEER_ET, d), lambda i, e: (e, 0)),
                  pl.BlockSpec((d, PEER_ET), lambda i, e: (0, e)),
                  pl.BlockSpec((d, t), lambda i, e: (0, i)),
                  plane_spec, plane_spec, table_spec, table_spec],
        out_specs=pl.BlockSpec((d, t), lambda i, e: (0, i)),
        out_shape=jax.ShapeDtypeStruct((d, n), F32),
        scratch_shapes=[pltpu.VMEM((PEER_ET, t), BF16)],
        compiler_params=_cparams(("parallel", "arbitrary")),
        name="peer_dense",
    )(u, vt, xnt, e2, r2, n1, w1)


def _add_t_kernel(x_ref, yt_ref, o_ref):
    o_ref[...] = x_ref[...] + yt_ref[...].T


def _add_t_norm_kernel(x_ref, yt_ref, g_ref, o_ref):
    o_ref[...] = _rms(x_ref[...] + yt_ref[...].T, g_ref[...])


def _add_transposed(x, yt, gain=None, tm=512):
    n, d = x.shape
    tm = min(tm, n)
    in_specs = [pl.BlockSpec((tm, d), lambda i: (i, 0)),
                pl.BlockSpec((d, tm), lambda i: (0, i))]
    args = [x, yt]
    body = _add_t_kernel
    if gain is not None:
        in_specs.append(pl.BlockSpec((1, d), lambda i: (0, 0)))
        args.append(gain.reshape(1, d))
        body = _add_t_norm_kernel
    return pl.pallas_call(
        body,
        grid=(n // tm,),
        in_specs=in_specs,
        out_specs=pl.BlockSpec((tm, d), lambda i: (i, 0)),
        out_shape=jax.ShapeDtypeStruct((n, d), F32),
        compiler_params=_cparams(("parallel",)),
        name="add_transposed",
    )(*args)


def _swap_halves(w):
    half = w.shape[-1] // 2
    return jnp.concatenate([w[..., half:], w[..., :half]], axis=-1)


def _prep_w_in(w_in):
    d = w_in.shape[0]
    kr = w_in[:, KR_OFF:D_IN]
    z = jnp.zeros((d, HEAD_DIM - C_ROPE), w_in.dtype)
    return jnp.concatenate([w_in[:, :KR_OFF], kr, z, _swap_halves(kr), z], axis=-1).astype(BF16)


def _prep_w_uq(w_uq):
    r = w_uq.shape[0]
    z = jnp.zeros((r, HEAD_DIM - C_ROPE), w_uq.dtype)
    blocks = []
    for h in range(C_HEADS):
        b = h * (C_NOPE + C_ROPE)
        rope = w_uq[:, b + C_NOPE:b + C_NOPE + C_ROPE]
        blocks += [w_uq[:, b:b + C_NOPE], rope, z, _swap_halves(rope), z]
    return jnp.concatenate(blocks, axis=-1).astype(BF16)


def _prep_w_ukv(w_ukv):
    ks = [w_ukv[:, h * (C_NOPE + C_V):h * (C_NOPE + C_V) + C_NOPE] for h in range(C_HEADS)]
    vs = [w_ukv[:, h * (C_NOPE + C_V) + C_NOPE:(h + 1) * (C_NOPE + C_V)] for h in range(C_HEADS)]
    return jnp.concatenate(ks + vs, axis=-1).astype(BF16)


def _mixer(x2, bsz, s_len, ln1, w_in, a_sink, b_rel_bias, c_q_norm, c_kv_norm, c_w_uq, c_w_ukv,
           out_norm, w_o):
    h2 = _norm_matmul(x2, ln1, _prep_w_in(w_in))
    h3 = h2.reshape(bsz, s_len, D_IN_PAD)
    oa = _attn_a(h3, a_sink.astype(F32), out_norm[:A_Q])
    ob = _attn_b(h3, b_rel_bias, out_norm[A_Q:A_Q + B_W])
    qc, kc, vc = _mla_proj(h2, s_len, c_q_norm, c_kv_norm, _prep_w_uq(c_w_uq), _prep_w_ukv(c_w_ukv))
    oc = _mla_attn(qc.reshape(bsz, s_len, -1), kc.reshape(bsz, s_len, -1),
                   vc.reshape(bsz, s_len, -1), out_norm[A_Q + B_W:])
    n = bsz * s_len
    return _out_proj(x2, oa.reshape(n, A_Q), ob.reshape(n, B_W), oc.reshape(n, C_OUT),
                     w_o.astype(BF16))


def _peer(x2, ln2, w_q, sub_keys, u, vv):
    keys = sub_keys.astype(F32)
    keys = jnp.stack([keys[0], jnp.concatenate([keys[1, 0::2], keys[1, 1::2]], axis=0)])
    xnt, e2, r2, n1, w1 = _peer_route(x2, ln2, w_q.T.astype(BF16), keys)
    return _peer_dense(u.astype(BF16), vv.T.astype(BF16), xnt, e2, r2, n1, w1)


def kernel(x, ln1, w_in, a_sink, b_rel_bias, c_q_norm, c_kv_norm, c_w_uq, c_w_ukv, out_norm, w_o,
           ln2, peer_w_q, peer_sub_keys, peer_u, peer_v, final_norm):
    bsz, s_len, d = x.shape
    depth = ln1.shape[0]
    x2 = x.reshape(bsz * s_len, d)
    for l in range(depth):
        x2 = _mixer(x2, bsz, s_len, ln1[l], w_in[l], a_sink[l], b_rel_bias[l], c_q_norm[l],
                    c_kv_norm[l], c_w_uq[l], c_w_ukv[l], out_norm[l], w_o[l])
        yt = _peer(x2, ln2[l], peer_w_q[l], peer_sub_keys[l], peer_u[l], peer_v[l])
        x2 = _add_transposed(x2, yt, final_norm if l == depth - 1 else None)
    return x2.reshape(bsz, s_len, d)
```

```python
import functools
import math

import jax
import jax.numpy as jnp
import numpy as np
from jax import lax
from jax.experimental import pallas as pl
from jax.experimental.pallas import tpu as pltpu

F32 = jnp.float32
BF16 = jnp.bfloat16

EPS = 1e-6
HEAD_DIM = 128
A_HEADS = 8
A_KV_HEADS = 2
A_GROUP = A_HEADS // A_KV_HEADS
A_WINDOW = 128
B_HEADS = 4
GRID_W = 64
NB_ROWS = 8
NB_COLS = 16
C_HEADS = 4
C_Q_RANK = 512
C_KV_RANK = 256
C_NOPE = 128
C_ROPE = 64
C_V = 128
ROPE_THETA = 10000.0
A_Q = A_HEADS * HEAD_DIM
A_KV = A_KV_HEADS * HEAD_DIM
B_W = B_HEADS * HEAD_DIM
C_OUT = C_HEADS * C_V
D_MIX = A_Q + B_W + C_OUT
D_IN = A_Q + 2 * A_KV + 3 * B_W + C_Q_RANK + C_KV_RANK + C_ROPE
D_IN_PAD = 4096
KR_OFF = D_IN - C_ROPE
PEER_HEADS = 8
PEER_KEYS = 128
PEER_HALF = 64
PEER_TOPK = 16

VMEM_LIMIT = 56 * 1024 * 1024
NEG_INF = float("-inf")


def _cparams(sem):
    return pltpu.CompilerParams(dimension_semantics=sem, vmem_limit_bytes=VMEM_LIMIT)


def _rms(x, g):
    ms = jnp.mean(x * x, axis=-1, keepdims=True)
    return x * lax.rsqrt(ms + EPS) * g


def _norm_matmul_kernel(x_ref, g_ref, w_ref, o_ref, xn_ref):
    @pl.when(pl.program_id(1) == 0)
    def _():
        xn_ref[...] = _rms(x_ref[...], g_ref[...]).astype(BF16)

    o_ref[...] = jnp.dot(xn_ref[...], w_ref[...],
                         preferred_element_type=F32).astype(o_ref.dtype)


def _norm_matmul(x, g, w, tm=1024, tn=512):
    n, d = x.shape
    nout = w.shape[1]
    tm = min(tm, n)
    return pl.pallas_call(
        _norm_matmul_kernel,
        grid=(n // tm, nout // tn),
        in_specs=[pl.BlockSpec((tm, d), lambda i, j: (i, 0)),
                  pl.BlockSpec((1, d), lambda i, j: (0, 0)),
                  pl.BlockSpec((d, tn), lambda i, j: (0, j))],
        out_specs=pl.BlockSpec((tm, tn), lambda i, j: (i, j)),
        out_shape=jax.ShapeDtypeStruct((n, nout), BF16),
        scratch_shapes=[pltpu.VMEM((tm, d), BF16)],
        compiler_params=_cparams(("parallel", "arbitrary")),
        name="norm_matmul",
    )(x, g.reshape(1, d), w)


A_TQ = 512
A_BAND = 3 * A_WINDOW


def _attn_a_kernel(sink_ref, q_ref, k_ref, v_ref, g_ref, o_ref):
    i = pl.program_id(1)
    s_len = k_ref.shape[0]
    scale = HEAD_DIM ** -0.5
    slopes = [2.0 ** (-8.0 * (h + 1) / A_HEADS) for h in range(A_HEADS)]
    for j in range(A_TQ // A_WINDOW):
        q0 = i * A_TQ + j * A_WINDOW
        start = pl.multiple_of(jnp.clip(q0 - A_WINDOW, 0, s_len - A_BAND), A_WINDOW)
        kb = k_ref[pl.ds(start, A_BAND), :]
        vb = v_ref[pl.ds(start, A_BAND), :]
        q_pos = q0 + lax.broadcasted_iota(jnp.int32, (A_WINDOW, A_BAND), 0)
        k_pos = start + lax.broadcasted_iota(jnp.int32, (A_WINDOW, A_BAND), 1)
        dist = jnp.abs(q_pos - k_pos)
        valid = dist <= A_WINDOW
        dist_f = dist.astype(F32)
        outs = []
        ssq = jnp.zeros((A_WINDOW, 1), F32)
        for kh in range(A_KV_HEADS):
            kk = kb[:, kh * HEAD_DIM:(kh + 1) * HEAD_DIM]
            vv = vb[:, kh * HEAD_DIM:(kh + 1) * HEAD_DIM]
            for g in range(A_GROUP):
                h = kh * A_GROUP + g
                qh = q_ref[j * A_WINDOW:(j + 1) * A_WINDOW, h * HEAD_DIM:(h + 1) * HEAD_DIM]
                s = lax.dot_general(qh, kk, (((1,), (1,)), ((), ())),
                                    preferred_element_type=F32) * scale
                s = s - slopes[h] * dist_f
                s = jnp.where(valid, s, NEG_INF)
                sink = sink_ref[h]
                m = jnp.maximum(jnp.max(s, axis=-1, keepdims=True), sink)
                p = jnp.exp(s - m)
                denom = jnp.sum(p, axis=-1, keepdims=True) + jnp.exp(sink - m)
                p = p / denom
                o = jnp.dot(p.astype(BF16), vv, preferred_element_type=F32)
                ssq = ssq + jnp.sum(o * o, axis=-1, keepdims=True)
                outs.append(o)
        inv = lax.rsqrt(ssq / A_Q + EPS)
        for h in range(A_HEADS):
            cols = slice(h * HEAD_DIM, (h + 1) * HEAD_DIM)
            o_ref[j * A_WINDOW:(j + 1) * A_WINDOW, cols] = (
                outs[h] * inv * g_ref[:, cols]).astype(o_ref.dtype)


def _attn_a(h3, sink, gain):
    bsz, s_len, _ = h3.shape
    assert s_len % A_TQ == 0 and s_len >= A_BAND
    kcol = A_Q // A_KV
    return pl.pallas_call(
        _attn_a_kernel,
        grid=(bsz, s_len // A_TQ),
        in_specs=[pl.BlockSpec(memory_space=pltpu.SMEM),
                  pl.BlockSpec((None, A_TQ, A_Q), lambda b, i: (b, i, 0)),
                  pl.BlockSpec((None, s_len, A_KV), lambda b, i: (b, 0, kcol)),
                  pl.BlockSpec((None, s_len, A_KV), lambda b, i: (b, 0, kcol + 1)),
                  pl.BlockSpec((1, A_Q), lambda b, i: (0, 0))],
        out_specs=pl.BlockSpec((None, A_TQ, A_Q), lambda b, i: (b, i, 0)),
        out_shape=jax.ShapeDtypeStruct((bsz, s_len, A_Q), BF16),
        compiler_params=_cparams(("parallel", "arbitrary")),
        name="attn_a",
    )(sink, h3, h3, h3, gain.reshape(1, A_Q))


B_RQ = 8
B_KEYS = NB_ROWS * GRID_W
B_QCOL = (A_Q + 2 * A_KV) // B_W


def _attn_b_kernel(q_ref, k_ref, v_ref, bias_ref, g_ref, o_ref):
    i = pl.program_id(1)
    rows = k_ref.shape[0] // GRID_W
    scale = HEAD_DIM ** -0.5
    for jr in range(B_RQ):
        r = i * B_RQ + jr
        r_start = jnp.clip(r - NB_ROWS // 2, 0, rows - NB_ROWS)
        off = r_start - r + NB_ROWS - 1
        kstart = pl.multiple_of(r_start * GRID_W, GRID_W)
        kb = k_ref[pl.ds(kstart, B_KEYS), :]
        vb = v_ref[pl.ds(kstart, B_KEYS), :]
        hs = [slice(h * HEAD_DIM, (h + 1) * HEAD_DIM) for h in range(B_HEADS)]
        q3 = jnp.stack([q_ref[jr * GRID_W:(jr + 1) * GRID_W, c] for c in hs])
        k3 = jnp.stack([kb[:, c] for c in hs])
        v3 = jnp.stack([vb[:, c] for c in hs])
        s = jnp.einsum('hqd,hkd->hqk', q3, k3, preferred_element_type=F32) * scale + bias_ref[off]
        m = jnp.max(s, axis=-1, keepdims=True)
        p = jnp.exp(s - m)
        p = p / jnp.sum(p, axis=-1, keepdims=True)
        o = jnp.einsum('hqk,hkd->hqd', p.astype(BF16), v3, preferred_element_type=F32)
        ssq = jnp.sum(jnp.sum(o * o, axis=-1, keepdims=True), axis=0)
        outs = [o[h] for h in range(B_HEADS)]
        inv = lax.rsqrt(ssq / B_W + EPS)
        for h in range(B_HEADS):
            cols = slice(h * HEAD_DIM, (h + 1) * HEAD_DIM)
            o_ref[jr * GRID_W:(jr + 1) * GRID_W, cols] = (
                outs[h] * inv * g_ref[:, cols]).astype(o_ref.dtype)


def _nb_bias_table(rel_bias):
    col = jnp.arange(GRID_W)
    c_start = jnp.clip(col - NB_COLS // 2, 0, GRID_W - NB_COLS)
    col_ok = (col[None, :] >= c_start[:, None]) & (col[None, :] < c_start[:, None] + NB_COLS)
    dc_idx = jnp.clip(col[None, :] - col[:, None] + NB_COLS - 1, 0, 2 * NB_COLS - 2)
    rb = rel_bias.astype(F32)[:, :, dc_idx]
    rb = jnp.where(col_ok[None, None], rb, NEG_INF)
    tabs = []
    for off in range(NB_ROWS):
        t = jnp.transpose(rb[:, off:off + NB_ROWS], (0, 2, 1, 3))
        tabs.append(t.reshape(B_HEADS, GRID_W, B_KEYS))
    return jnp.stack(tabs)


def _attn_b(h3, rel_bias, gain):
    bsz, s_len, _ = h3.shape
    rows = s_len // GRID_W
    assert rows >= NB_ROWS and rows % B_RQ == 0
    tq = B_RQ * GRID_W
    bias = _nb_bias_table(rel_bias)
    return pl.pallas_call(
        _attn_b_kernel,
        grid=(bsz, rows // B_RQ),
        in_specs=[pl.BlockSpec((None, tq, B_W), lambda b, i: (b, i, B_QCOL)),
                  pl.BlockSpec((None, s_len, B_W), lambda b, i: (b, 0, B_QCOL + 1)),
                  pl.BlockSpec((None, s_len, B_W), lambda b, i: (b, 0, B_QCOL + 2)),
                  pl.BlockSpec((NB_ROWS, B_HEADS, GRID_W, B_KEYS), lambda b, i: (0, 0, 0, 0)),
                  pl.BlockSpec((1, B_W), lambda b, i: (0, 0))],
        out_specs=pl.BlockSpec((None, tq, B_W), lambda b, i: (b, i, 0)),
        out_shape=jax.ShapeDtypeStruct((bsz, s_len, B_W), BF16),
        compiler_params=_cparams(("parallel", "arbitrary")),
        name="attn_b",
    )(h3, h3, h3, bias, gain.reshape(1, B_W))


C_QK = 256
C_QW = 3 * HEAD_DIM
MLA_TM = 512
MLA_TQ = 256


def _mla_proj_kernel(cq_ref, ckv_ref, kr_ref, gq_ref, gkv_ref, wq_ref, wkv_ref,
                     cos_ref, sin_ref, q_ref, k_ref, v_ref):
    cos = cos_ref[...]
    sin = sin_ref[...]
    cqn = _rms(cq_ref[...].astype(F32), gq_ref[...]).astype(BF16)
    qf = jnp.dot(cqn, wq_ref[...], preferred_element_type=F32)
    ckvn = _rms(ckv_ref[...].astype(F32), gkv_ref[...]).astype(BF16)
    kvf = jnp.dot(ckvn, wkv_ref[...], preferred_element_type=F32)
    kr = kr_ref[...].astype(F32)
    kpe = (kr[:, :HEAD_DIM] * cos + kr[:, HEAD_DIM:] * sin).astype(k_ref.dtype)
    for h in range(C_HEADS):
        b = h * C_QW
        q_ref[:, h * C_QK:h * C_QK + C_NOPE] = qf[:, b:b + C_NOPE].astype(q_ref.dtype)
        rope = qf[:, b + HEAD_DIM:b + 2 * HEAD_DIM] * cos + qf[:, b + 2 * HEAD_DIM:b + 3 * HEAD_DIM] * sin
        q_ref[:, h * C_QK + C_NOPE:(h + 1) * C_QK] = rope.astype(q_ref.dtype)
        k_ref[:, h * C_QK:h * C_QK + C_NOPE] = kvf[:, h * C_NOPE:(h + 1) * C_NOPE].astype(k_ref.dtype)
        k_ref[:, h * C_QK + C_NOPE:(h + 1) * C_QK] = kpe
    v_ref[...] = kvf[:, C_HEADS * C_NOPE:].astype(v_ref.dtype)


def _rope_tables(s_len):
    half = C_ROPE // 2
    inv = ROPE_THETA ** (-jnp.arange(half, dtype=F32) / half)
    ang = jnp.arange(s_len, dtype=F32)[:, None] * inv[None, :]
    cos = jnp.cos(ang)
    sin = jnp.sin(ang)
    zeros = jnp.zeros((s_len, HEAD_DIM - C_ROPE), F32)
    return (jnp.concatenate([cos, cos, zeros], axis=-1),
            jnp.concatenate([-sin, sin, zeros], axis=-1))


def _mla_proj(h2, s_len, gq, gkv, wq, wkv):
    n = h2.shape[0]
    tm = min(MLA_TM, s_len)
    cos, sin = _rope_tables(s_len)
    per_seq = s_len // tm
    qk_shape = jax.ShapeDtypeStruct((n, C_HEADS * C_QK), BF16)
    cq_col = (D_IN - C_ROPE - C_KV_RANK - C_Q_RANK) // C_Q_RANK
    ckv_col = (D_IN - C_ROPE - C_KV_RANK) // C_KV_RANK
    kr_col = KR_OFF // (2 * HEAD_DIM)
    return pl.pallas_call(
        _mla_proj_kernel,
        grid=(n // tm,),
        in_specs=[pl.BlockSpec((tm, C_Q_RANK), lambda i: (i, cq_col)),
                  pl.BlockSpec((tm, C_KV_RANK), lambda i: (i, ckv_col)),
                  pl.BlockSpec((tm, 2 * HEAD_DIM), lambda i: (i, kr_col)),
                  pl.BlockSpec((1, C_Q_RANK), lambda i: (0, 0)),
                  pl.BlockSpec((1, C_KV_RANK), lambda i: (0, 0)),
                  pl.BlockSpec(wq.shape, lambda i: (0, 0)),
                  pl.BlockSpec(wkv.shape, lambda i: (0, 0)),
                  pl.BlockSpec((tm, HEAD_DIM), lambda i: (i % per_seq, 0)),
                  pl.BlockSpec((tm, HEAD_DIM), lambda i: (i % per_seq, 0))],
        out_specs=[pl.BlockSpec((tm, C_HEADS * C_QK), lambda i: (i, 0)),
                   pl.BlockSpec((tm, C_HEADS * C_QK), lambda i: (i, 0)),
                   pl.BlockSpec((tm, C_OUT), lambda i: (i, 0))],
        out_shape=[qk_shape, qk_shape, jax.ShapeDtypeStruct((n, C_OUT), BF16)],
        compiler_params=_cparams(("parallel",)),
        name="mla_proj",
    )(h2, h2, h2, gq.reshape(1, -1), gkv.reshape(1, -1), wq, wkv, cos, sin)


def _mla_attn_kernel(q_ref, k_ref, v_ref, g_ref, o_ref):
    scale = (C_NOPE + C_ROPE) ** -0.5
    outs = []
    ssq = jnp.zeros((q_ref.shape[0], 1), F32)
    for h in range(C_HEADS):
        qk = slice(h * C_QK, (h + 1) * C_QK)
        s = lax.dot_general(q_ref[:, qk], k_ref[:, qk], (((1,), (1,)), ((), ())),
                            preferred_element_type=F32) * scale
        m = jnp.max(s, axis=-1, keepdims=True)
        p = jnp.exp(s - m)
        p = p / jnp.sum(p, axis=-1, keepdims=True)
        o = jnp.dot(p.astype(BF16), v_ref[:, h * C_V:(h + 1) * C_V],
                    preferred_element_type=F32)
        ssq = ssq + jnp.sum(o * o, axis=-1, keepdims=True)
        outs.append(o)
    inv = lax.rsqrt(ssq / C_OUT + EPS)
    for h in range(C_HEADS):
        cols = slice(h * C_V, (h + 1) * C_V)
        o_ref[:, cols] = (outs[h] * inv * g_ref[:, cols]).astype(o_ref.dtype)


def _mla_attn(q3, k3, v3, gain):
    bsz, s_len, _ = q3.shape
    tq = min(MLA_TQ, s_len)
    return pl.pallas_call(
        _mla_attn_kernel,
        grid=(bsz, s_len // tq),
        in_specs=[pl.BlockSpec((None, tq, C_HEADS * C_QK), lambda b, i: (b, i, 0)),
                  pl.BlockSpec((None, s_len, C_HEADS * C_QK), lambda b, i: (b, 0, 0)),
                  pl.BlockSpec((None, s_len, C_OUT), lambda b, i: (b, 0, 0)),
                  pl.BlockSpec((1, C_OUT), lambda b, i: (0, 0))],
        out_specs=pl.BlockSpec((None, tq, C_OUT), lambda b, i: (b, i, 0)),
        out_shape=jax.ShapeDtypeStruct((bsz, s_len, C_OUT), BF16),
        compiler_params=_cparams(("parallel", "arbitrary")),
        name="mla_attn",
    )(q3, k3, v3, gain.reshape(1, C_OUT))


def _out_proj_kernel(x_ref, oa_ref, ob_ref, oc_ref, wa_ref, wb_ref, wc_ref, o_ref):
    acc = jnp.dot(oa_ref[...], wa_ref[...], preferred_element_type=F32)
    acc = acc + jnp.dot(ob_ref[...], wb_ref[...], preferred_element_type=F32)
    acc = acc + jnp.dot(oc_ref[...], wc_ref[...], preferred_element_type=F32)
    o_ref[...] = x_ref[...] + acc


def _out_proj(x, oa, ob, oc, w_o, tm=1024, tn=512):
    n, d = x.shape
    tm = min(tm, n)
    return pl.pallas_call(
        _out_proj_kernel,
        grid=(n // tm, d // tn),
        in_specs=[pl.BlockSpec((tm, tn), lambda i, j: (i, j)),
                  pl.BlockSpec((tm, A_Q), lambda i, j: (i, 0)),
                  pl.BlockSpec((tm, B_W), lambda i, j: (i, 0)),
                  pl.BlockSpec((tm, C_OUT), lambda i, j: (i, 0)),
                  pl.BlockSpec((A_Q, tn), lambda i, j: (0, j)),
                  pl.BlockSpec((B_W, tn), lambda i, j: (A_Q // B_W, j)),
                  pl.BlockSpec((C_OUT, tn), lambda i, j: ((A_Q + B_W) // C_OUT, j))],
        out_specs=pl.BlockSpec((tm, tn), lambda i, j: (i, j)),
        out_shape=jax.ShapeDtypeStruct((n, d), F32),
        compiler_params=_cparams(("parallel", "arbitrary")),
        name="out_proj",
    )(x, oa, ob, oc, w_o, w_o, w_o)


ROUTE_TM = 256
_CAND_ROWS = [(a, PEER_TOPK // (a + 1)) for a in range(PEER_TOPK)]
SUBLANES = 8


def _top_ranks(s, idx, exact):
    rank = jnp.full(s.shape, float(PEER_KEYS), F32)
    vals = []
    for k in range(PEER_TOPK):
        m = jnp.max(s, axis=0, keepdims=True)
        hit = s == m
        if exact:
            first = jnp.min(jnp.where(hit, idx, float(PEER_KEYS)), axis=0, keepdims=True)
            hit = idx == first
        rank = jnp.where(hit, float(k), rank)
        s = jnp.where(hit, NEG_INF, s)
        vals.append(m)
    return rank, vals


def _ranked_count(rank):
    return jnp.sum(jnp.where(rank < float(PEER_TOPK), 1.0, 0.0), axis=0, keepdims=True)


def _bf16_bits_high(x):
    return pltpu.bitcast(x.astype(BF16).astype(F32), jnp.uint32)


def _pack_halves(x):
    half = x.shape[0] // 2
    return (_bf16_bits_high(x[:half]) >> 16) | _bf16_bits_high(x[half:])


def _pack_twice(x):
    bits = _bf16_bits_high(x)
    return (bits >> 16) | bits


def _peer_route_kernel(x_ref, g_ref, wq_ref, keys_ref, xnt_ref, e2_ref, r2_ref, n1_ref, w1_ref):
    tm = x_ref.shape[0]
    xn = _rms(x_ref[...], g_ref[...])
    xnt = xn.T.astype(BF16)
    xnt_ref[...] = xnt
    qt = jnp.dot(wq_ref[...], xnt, preferred_element_type=F32)
    row = lax.broadcasted_iota(jnp.int32, (PEER_KEYS, tm), 0).astype(F32)
    row2 = jnp.where(row < PEER_KEYS // 2, 2.0 * row, 2.0 * row - (PEER_KEYS - 1))

    def route_head(h, exact):
        sc = []
        for c in range(2):
            row0 = (h * 2 + c) * PEER_HALF
            sc.append(jnp.dot(keys_ref[c], qt[row0:row0 + PEER_HALF, :],
                              preferred_element_type=F32,
                              precision=lax.Precision.HIGHEST))
        rank1, t1 = _top_ranks(sc[0], row, exact)
        rank2, t2 = _top_ranks(sc[1], row2, exact)
        e1 = [jnp.exp(t1[a] - t1[0]) for a in range(PEER_TOPK)]
        e2 = [jnp.exp(t2[b] - t2[0]) for b in range(PEER_TOPK)]
        groups, gmeta = [], []
        for a, nb in _CAND_ROWS:
            for b0 in range(0, nb, SUBLANES):
                bs = list(range(b0, min(b0 + SUBLANES, nb)))
                rows = [t1[a] + t2[b] for b in bs]
                rows += [jnp.full((1, tm), NEG_INF, F32)] * (SUBLANES - len(bs))
                groups.append(jnp.concatenate(rows, axis=0))
                gmeta.append((a, bs))
        cand = jnp.concatenate(groups, axis=0)
        cidx = lax.broadcasted_iota(jnp.int32, cand.shape, 0).astype(F32)
        crank, _ = _top_ranks(cand, cidx, exact)
        self_f = jnp.where(crank < float(PEER_TOPK), 1.0, 0.0)
        n_a = [jnp.zeros((1, tm), F32) for _ in range(PEER_TOPK)]
        z = jnp.zeros((1, tm), F32)
        for gi, (a, bs) in enumerate(gmeta):
            blk = self_f[gi * SUBLANES:(gi + 1) * SUBLANES, :]
            n_a[a] = n_a[a] + jnp.sum(blk, axis=0, keepdims=True)
            for bi, b in enumerate(bs):
                z = z + blk[bi:bi + 1, :] * (e1[a] * e2[b])
        n1 = jnp.zeros(rank1.shape, F32)
        for a in range(PEER_TOPK):
            n1 = jnp.where(rank1 == float(a), n_a[a], n1)
        e2_ref[h] = _pack_halves(jnp.exp(sc[1] - t2[0]))
        r2_ref[h] = _pack_halves(rank2)
        n1_ref[h] = _pack_twice(n1).reshape(PEER_KEYS // SUBLANES, SUBLANES, tm)
        w1_ref[h] = _pack_twice(jnp.exp(sc[0] - t1[0]) / z).reshape(PEER_KEYS // SUBLANES, SUBLANES, tm)
        if exact:
            return jnp.zeros((1, tm), F32)
        counts = [_ranked_count(rank1), _ranked_count(rank2), jnp.sum(self_f, axis=0, keepdims=True)]
        return sum(jnp.where(c == float(PEER_TOPK), 0.0, 1.0) for c in counts)

    def redo_head(h):
        route_head(h, exact=True)

    for h in range(PEER_HEADS):
        tied = route_head(h, exact=False)
        pl.when(jnp.max(tied) > 0.0)(functools.partial(redo_head, h))


def _peer_route(x, g, wq_t, sub_keys):
    n, d = x.shape
    tm = min(ROUTE_TM, n)
    plane = jax.ShapeDtypeStruct((PEER_HEADS, PEER_KEYS // 2, n), jnp.uint32)
    plane_spec = pl.BlockSpec((PEER_HEADS, PEER_KEYS // 2, tm), lambda i: (0, 0, i))
    table = jax.ShapeDtypeStruct((PEER_HEADS, PEER_KEYS // SUBLANES, SUBLANES, n), jnp.uint32)
    table_spec = pl.BlockSpec((PEER_HEADS, PEER_KEYS // SUBLANES, SUBLANES, tm),
                              lambda i: (0, 0, 0, i))
    return pl.pallas_call(
        _peer_route_kernel,
        grid=(n // tm,),
        in_specs=[pl.BlockSpec((tm, d), lambda i: (i, 0)),
                  pl.BlockSpec((1, d), lambda i: (0, 0)),
                  pl.BlockSpec(wq_t.shape, lambda i: (0, 0)),
                  pl.BlockSpec(sub_keys.shape, lambda i: (0, 0, 0))],
        out_specs=[pl.BlockSpec((d, tm), lambda i: (0, i)),
                   plane_spec, plane_spec, table_spec, table_spec],
        out_shape=[jax.ShapeDtypeStruct((d, n), BF16), plane, plane, table, table],
        compiler_params=_cparams(("parallel",)),
        name="peer_route",
    )(x, g.reshape(1, d), wq_t, sub_keys)


PEER_T = 512
PEER_ET = SUBLANES * PEER_KEYS
LANES = 128


def _rows_bf16(words):
    return pltpu.bitcast(jnp.broadcast_to(words, (PEER_KEYS // 2, LANES)), BF16)


def _peer_dense_kernel(u_ref, vt_ref, xnt_ref, e2_ref, r2_ref, n1_ref, w1_ref, yt_ref, h_ref):
    e = pl.program_id(1)
    t = xnt_ref.shape[1]

    @pl.when(e == 0)
    def _():
        yt_ref[...] = jnp.zeros_like(yt_ref)

    at = jnp.dot(u_ref[...], xnt_ref[...], preferred_element_type=F32)
    for rl in range(SUBLANES):
        for tc in range(t // LANES):
            cols = slice(tc * LANES, (tc + 1) * LANES)
            gate = jnp.zeros((PEER_KEYS, LANES), BF16)
            for h in range(PEER_HEADS):
                n1r = _rows_bf16(n1_ref[h, e, rl:rl + 1, cols])
                w1r = _rows_bf16(w1_ref[h, e, rl:rl + 1, cols])
                r2 = pltpu.bitcast(r2_ref[h, :, cols], BF16)
                e2 = pltpu.bitcast(e2_ref[h, :, cols], BF16)
                gate = gate + jnp.where(r2 < n1r, e2 * w1r, jnp.zeros((), BF16))
            a = at[rl * PEER_KEYS:(rl + 1) * PEER_KEYS, cols]
            h_ref[rl * PEER_KEYS:(rl + 1) * PEER_KEYS, cols] = jax.nn.gelu(a.astype(BF16)) * gate
    yt_ref[...] += jnp.dot(vt_ref[...], h_ref[...], preferred_element_type=F32)


def _peer_dense(u, vt, xnt, e2, r2, n1, w1):
    n_exp, d = u.shape
    n = xnt.shape[1]
    t = min(PEER_T, n)
    plane_spec = pl.BlockSpec((PEER_HEADS, PEER_KEYS // 2, t), lambda i, e: (0, 0, i))
    table_spec = pl.BlockSpec((PEER_HEADS, PEER_KEYS // SUBLANES, SUBLANES, t),
                              lambda i, e: (0, 0, 0, i))
    return pl.pallas_call(
        _peer_dense_kernel,
        grid=(n // t, n_exp // PEER_ET),
        in_specs=[pl.BlockSpec((PEER_ET, d), lambda i, e: (e, 0)),
                  pl.BlockSpec((d, PEER_ET), lambda i, e: (0, e)),
                  pl.BlockSpec((d, t), lambda i, e: (0, i)),
                  plane_spec, plane_spec, table_spec, table_spec],
        out_specs=pl.BlockSpec((d, t), lambda i, e: (0, i)),
        out_shape=jax.ShapeDtypeStruct((d, n), F32),
        scratch_shapes=[pltpu.VMEM((PEER_ET, t), BF16)],
        compiler_params=_cparams(("parallel", "arbitrary")),
        name="peer_dense",
    )(u, vt, xnt, e2, r2, n1, w1)


def _add_t_kernel(x_ref, yt_ref, o_ref):
    o_ref[...] = x_ref[...] + yt_ref[...].T


def _add_t_norm_kernel(x_ref, yt_ref, g_ref, o_ref):
    o_ref[...] = _rms(x_ref[...] + yt_ref[...].T, g_ref[...])


def _add_transposed(x, yt, gain=None, tm=512):
    n, d = x.shape
    tm = min(tm, n)
    in_specs = [pl.BlockSpec((tm, d), lambda i: (i, 0)),
                pl.BlockSpec((d, tm), lambda i: (0, i))]
    args = [x, yt]
    body = _add_t_kernel
    if gain is not None:
        in_specs.append(pl.BlockSpec((1, d), lambda i: (0, 0)))
        args.append(gain.reshape(1, d))
        body = _add_t_norm_kernel
    return pl.pallas_call(
        body,
        grid=(n // tm,),
        in_specs=in_specs,
        out_specs=pl.BlockSpec((tm, d), lambda i: (i, 0)),
        out_shape=jax.ShapeDtypeStruct((n, d), F32),
        compiler_params=_cparams(("parallel",)),
        name="add_transposed",
    )(*args)


def _swap_halves(w):
    half = w.shape[-1] // 2
    return jnp.concatenate([w[..., half:], w[..., :half]], axis=-1)


def _prep_w_in(w_in):
    d = w_in.shape[0]
    kr = w_in[:, KR_OFF:D_IN]
    z = jnp.zeros((d, HEAD_DIM - C_ROPE), w_in.dtype)
    return jnp.concatenate([w_in[:, :KR_OFF], kr, z, _swap_halves(kr), z], axis=-1).astype(BF16)


def _prep_w_uq(w_uq):
    r = w_uq.shape[0]
    z = jnp.zeros((r, HEAD_DIM - C_ROPE), w_uq.dtype)
    blocks = []
    for h in range(C_HEADS):
        b = h * (C_NOPE + C_ROPE)
        rope = w_uq[:, b + C_NOPE:b + C_NOPE + C_ROPE]
        blocks += [w_uq[:, b:b + C_NOPE], rope, z, _swap_halves(rope), z]
    return jnp.concatenate(blocks, axis=-1).astype(BF16)


def _prep_w_ukv(w_ukv):
    ks = [w_ukv[:, h * (C_NOPE + C_V):h * (C_NOPE + C_V) + C_NOPE] for h in range(C_HEADS)]
    vs = [w_ukv[:, h * (C_NOPE + C_V) + C_NOPE:(h + 1) * (C_NOPE + C_V)] for h in range(C_HEADS)]
    return jnp.concatenate(ks + vs, axis=-1).astype(BF16)


def _mixer(x2, bsz, s_len, ln1, w_in, a_sink, b_rel_bias, c_q_norm, c_kv_norm, c_w_uq, c_w_ukv,
           out_norm, w_o):
    h2 = _norm_matmul(x2, ln1, _prep_w_in(w_in))
    h3 = h2.reshape(bsz, s_len, D_IN_PAD)
    oa = _attn_a(h3, a_sink.astype(F32), out_norm[:A_Q])
    ob = _attn_b(h3, b_rel_bias, out_norm[A_Q:A_Q + B_W])
    qc, kc, vc = _mla_proj(h2, s_len, c_q_norm, c_kv_norm, _prep_w_uq(c_w_uq), _prep_w_ukv(c_w_ukv))
    oc = _mla_attn(qc.reshape(bsz, s_len, -1), kc.reshape(bsz, s_len, -1),
                   vc.reshape(bsz, s_len, -1), out_norm[A_Q + B_W:])
    n = bsz * s_len
    return _out_proj(x2, oa.reshape(n, A_Q), ob.reshape(n, B_W), oc.reshape(n, C_OUT),
                     w_o.astype(BF16))


def _peer(x2, ln2, w_q, sub_keys, u, vv):
    keys = sub_keys.astype(F32)
    keys = jnp.stack([keys[0], jnp.concatenate([keys[1, 0::2], keys[1, 1::2]], axis=0)])
    xnt, e2, r2, n1, w1 = _peer_route(x2, ln2, w_q.T.astype(BF16), keys)
    return _peer_dense(u.astype(BF16), vv.T.astype(BF16), xnt, e2, r2, n1, w1)


def kernel(x, ln1, w_in, a_sink, b_rel_bias, c_q_norm, c_kv_norm, c_w_uq, c_w_ukv, out_norm, w_o,
           ln2, peer_w_q, peer_sub_keys, peer_u, peer_v, final_norm):
    bsz, s_len, d = x.shape
    depth = ln1.shape[0]
    x2 = x.reshape(bsz * s_len, d)
    for l in range(depth):
        x2 = _mixer(x2, bsz, s_len, ln1[l], w_in[l], a_sink[l], b_rel_bias[l], c_q_norm[l],
                    c_kv_norm[l], c_w_uq[l], c_w_ukv[l], out_norm[l], w_o[l])
        yt = _peer(x2, ln2[l], peer_w_q[l], peer_sub_keys[l], peer_u[l], peer_v[l])
        x2 = _add_transposed(x2, yt, final_norm if l == depth - 1 else None)
    return x2.reshape(bsz, s_len, d)
```

```python
import functools
import math

import jax
import jax.numpy as jnp
import numpy as np
from jax import lax
from jax.experimental import pallas as pl
from jax.experimental.pallas import tpu as pltpu

F32 = jnp.float32
BF16 = jnp.bfloat16

EPS = 1e-6
HEAD_DIM = 128
A_HEADS = 8
A_KV_HEADS = 2
A_GROUP = A_HEADS // A_KV_HEADS
A_WINDOW = 128
B_HEADS = 4
GRID_W = 64
NB_ROWS = 8
NB_COLS = 16
C_HEADS = 4
C_Q_RANK = 512
C_KV_RANK = 256
C_NOPE = 128
C_ROPE = 64
C_V = 128
ROPE_THETA = 10000.0
A_Q = A_HEADS * HEAD_DIM
A_KV = A_KV_HEADS * HEAD_DIM
B_W = B_HEADS * HEAD_DIM
C_OUT = C_HEADS * C_V
D_MIX = A_Q + B_W + C_OUT
D_IN = A_Q + 2 * A_KV + 3 * B_W + C_Q_RANK + C_KV_RANK + C_ROPE
D_IN_PAD = 4096
KR_OFF = D_IN - C_ROPE
PEER_HEADS = 8
PEER_KEYS = 128
PEER_HALF = 64
PEER_TOPK = 16

VMEM_LIMIT = 56 * 1024 * 1024
NEG_INF = float("-inf")


def _cparams(sem):
    return pltpu.CompilerParams(dimension_semantics=sem, vmem_limit_bytes=VMEM_LIMIT)


def _rms(x, g):
    ms = jnp.mean(x * x, axis=-1, keepdims=True)
    return x * lax.rsqrt(ms + EPS) * g


def _norm_matmul_kernel(x_ref, g_ref, w_ref, o_ref, xn_ref):
    @pl.when(pl.program_id(1) == 0)
    def _():
        xn_ref[...] = _rms(x_ref[...], g_ref[...]).astype(BF16)

    o_ref[...] = jnp.dot(xn_ref[...], w_ref[...],
                         preferred_element_type=F32).astype(o_ref.dtype)


def _norm_matmul(x, g, w, tm=1024, tn=512):
    n, d = x.shape
    nout = w.shape[1]
    tm = min(tm, n)
    return pl.pallas_call(
        _norm_matmul_kernel,
        grid=(n // tm, nout // tn),
        in_specs=[pl.BlockSpec((tm, d), lambda i, j: (i, 0)),
                  pl.BlockSpec((1, d), lambda i, j: (0, 0)),
                  pl.BlockSpec((d, tn), lambda i, j: (0, j))],
        out_specs=pl.BlockSpec((tm, tn), lambda i, j: (i, j)),
        out_shape=jax.ShapeDtypeStruct((n, nout), BF16),
        scratch_shapes=[pltpu.VMEM((tm, d), BF16)],
        compiler_params=_cparams(("parallel", "arbitrary")),
        name="norm_matmul",
    )(x, g.reshape(1, d), w)


A_TQ = 512
A_BAND = 3 * A_WINDOW


def _attn_a_kernel(sink_ref, q_ref, k_ref, v_ref, g_ref, o_ref):
    i = pl.program_id(1)
    s_len = k_ref.shape[0]
    scale = HEAD_DIM ** -0.5
    slopes = [2.0 ** (-8.0 * (h + 1) / A_HEADS) for h in range(A_HEADS)]
    for j in range(A_TQ // A_WINDOW):
        q0 = i * A_TQ + j * A_WINDOW
        start = pl.multiple_of(jnp.clip(q0 - A_WINDOW, 0, s_len - A_BAND), A_WINDOW)
        kb = k_ref[pl.ds(start, A_BAND), :]
        vb = v_ref[pl.ds(start, A_BAND), :]
        q_pos = q0 + lax.broadcasted_iota(jnp.int32, (A_WINDOW, A_BAND), 0)
        k_pos = start + lax.broadcasted_iota(jnp.int32, (A_WINDOW, A_BAND), 1)
        dist = jnp.abs(q_pos - k_pos)
        valid = dist <= A_WINDOW
        dist_f = dist.astype(F32)
        outs = []
        ssq = jnp.zeros((A_WINDOW, 1), F32)
        for kh in range(A_KV_HEADS):
            kk = kb[:, kh * HEAD_DIM:(kh + 1) * HEAD_DIM]
            vv = vb[:, kh * HEAD_DIM:(kh + 1) * HEAD_DIM]
            for g in range(A_GROUP):
                h = kh * A_GROUP + g
                qh = q_ref[j * A_WINDOW:(j + 1) * A_WINDOW, h * HEAD_DIM:(h + 1) * HEAD_DIM]
                s = lax.dot_general(qh, kk, (((1,), (1,)), ((), ())),
                                    preferred_element_type=F32) * scale
                s = s - slopes[h] * dist_f
                s = jnp.where(valid, s, NEG_INF)
                sink = sink_ref[h]
                m = jnp.maximum(jnp.max(s, axis=-1, keepdims=True), sink)
                p = jnp.exp(s - m)
                denom = jnp.sum(p, axis=-1, keepdims=True) + jnp.exp(sink - m)
                p = p / denom
                o = jnp.dot(p.astype(BF16), vv, preferred_element_type=F32)
                ssq = ssq + jnp.sum(o * o, axis=-1, keepdims=True)
                outs.append(o)
        inv = lax.rsqrt(ssq / A_Q + EPS)
        for h in range(A_HEADS):
            cols = slice(h * HEAD_DIM, (h + 1) * HEAD_DIM)
            o_ref[j * A_WINDOW:(j + 1) * A_WINDOW, cols] = (
                outs[h] * inv * g_ref[:, cols]).astype(o_ref.dtype)


def _attn_a(h3, sink, gain):
    bsz, s_len, _ = h3.shape
    assert s_len % A_TQ == 0 and s_len >= A_BAND
    kcol = A_Q // A_KV
    return pl.pallas_call(
        _attn_a_kernel,
        grid=(bsz, s_len // A_TQ),
        in_specs=[pl.BlockSpec(memory_space=pltpu.SMEM),
                  pl.BlockSpec((None, A_TQ, A_Q), lambda b, i: (b, i, 0)),
                  pl.BlockSpec((None, s_len, A_KV), lambda b, i: (b, 0, kcol)),
                  pl.BlockSpec((None, s_len, A_KV), lambda b, i: (b, 0, kcol + 1)),
                  pl.BlockSpec((1, A_Q), lambda b, i: (0, 0))],
        out_specs=pl.BlockSpec((None, A_TQ, A_Q), lambda b, i: (b, i, 0)),
        out_shape=jax.ShapeDtypeStruct((bsz, s_len, A_Q), BF16),
        compiler_params=_cparams(("parallel", "arbitrary")),
        name="attn_a",
    )(sink, h3, h3, h3, gain.reshape(1, A_Q))


B_RQ = 8
B_KEYS = NB_ROWS * GRID_W
B_QCOL = (A_Q + 2 * A_KV) // B_W


def _attn_b_kernel(q_ref, k_ref, v_ref, bias_ref, g_ref, o_ref):
    i = pl.program_id(1)
    rows = k_ref.shape[0] // GRID_W
    scale = HEAD_DIM ** -0.5
    for jr in range(B_RQ):
        r = i * B_RQ + jr
        r_start = jnp.clip(r - NB_ROWS // 2, 0, rows - NB_ROWS)
        off = r_start - r + NB_ROWS - 1
        kstart = pl.multiple_of(r_start * GRID_W, GRID_W)
        kb = k_ref[pl.ds(kstart, B_KEYS), :]
        vb = v_ref[pl.ds(kstart, B_KEYS), :]
        hs = [slice(h * HEAD_DIM, (h + 1) * HEAD_DIM) for h in range(B_HEADS)]
        q3 = jnp.stack([q_ref[jr * GRID_W:(jr + 1) * GRID_W, c] for c in hs])
        k3 = jnp.stack([kb[:, c] for c in hs])
        v3 = jnp.stack([vb[:, c] for c in hs])
        s = jnp.einsum('hqd,hkd->hqk', q3, k3, preferred_element_type=F32) * scale + bias_ref[off]
        m = jnp.max(s, axis=-1, keepdims=True)
        p = jnp.exp(s - m)
        p = p / jnp.sum(p, axis=-1, keepdims=True)
        o = jnp.einsum('hqk,hkd->hqd', p.astype(BF16), v3, preferred_element_type=F32)
        ssq = jnp.sum(jnp.sum(o * o, axis=-1, keepdims=True), axis=0)
        outs = [o[h] for h in range(B_HEADS)]
        inv = lax.rsqrt(ssq / B_W + EPS)
        for h in range(B_HEADS):
            cols = slice(h * HEAD_DIM, (h + 1) * HEAD_DIM)
            o_ref[jr * GRID_W:(jr + 1) * GRID_W, cols] = (
                outs[h] * inv * g_ref[:, cols]).astype(o_ref.dtype)


def _nb_bias_table(rel_bias):
    col = jnp.arange(GRID_W)
    c_start = jnp.clip(col - NB_COLS // 2, 0, GRID_W - NB_COLS)
    col_ok = (col[None, :] >= c_start[:, None]) & (col[None, :] < c_start[:, None] + NB_COLS)
    dc_idx = jnp.clip(col[None, :] - col[:, None] + NB_COLS - 1, 0, 2 * NB_COLS - 2)
    rb = rel_bias.astype(F32)[:, :, dc_idx]
    rb = jnp.where(col_ok[None, None], rb, NEG_INF)
    tabs = []
    for off in range(NB_ROWS):
        t = jnp.transpose(rb[:, off:off + NB_ROWS], (0, 2, 1, 3))
        tabs.append(t.reshape(B_HEADS, GRID_W, B_KEYS))
    return jnp.stack(tabs)


def _attn_b(h3, rel_bias, gain):
    bsz, s_len, _ = h3.shape
    rows = s_len // GRID_W
    assert rows >= NB_ROWS and rows % B_RQ == 0
    tq = B_RQ * GRID_W
    bias = _nb_bias_table(rel_bias)
    return pl.pallas_call(
        _attn_b_kernel,
        grid=(bsz, rows // B_RQ),
        in_specs=[pl.BlockSpec((None, tq, B_W), lambda b, i: (b, i, B_QCOL)),
                  pl.BlockSpec((None, s_len, B_W), lambda b, i: (b, 0, B_QCOL + 1)),
                  pl.BlockSpec((None, s_len, B_W), lambda b, i: (b, 0, B_QCOL + 2)),
                  pl.BlockSpec((NB_ROWS, B_HEADS, GRID_W, B_KEYS), lambda b, i: (0, 0, 0, 0)),
                  pl.BlockSpec((1, B_W), lambda b, i: (0, 0))],
        out_specs=pl.BlockSpec((None, tq, B_W), lambda b, i: (b, i, 0)),
        out_shape=jax.ShapeDtypeStruct((bsz, s_len, B_W), BF16),
        compiler_params=_cparams(("parallel", "arbitrary")),
        name="attn_b",
    )(h3, h3, h3, bias, gain.reshape(1, B_W))


C_QK = 256
C_QW = 3 * HEAD_DIM
MLA_TM = 512
MLA_TQ = 256


def _mla_proj_kernel(cq_ref, ckv_ref, kr_ref, gq_ref, gkv_ref, wq_ref, wkv_ref,
                     cos_ref, sin_ref, q_ref, k_ref, v_ref):
    cos = cos_ref[...]
    sin = sin_ref[...]
    cqn = _rms(cq_ref[...].astype(F32), gq_ref[...]).astype(BF16)
    qf = jnp.dot(cqn, wq_ref[...], preferred_element_type=F32)
    ckvn = _rms(ckv_ref[...].astype(F32), gkv_ref[...]).astype(BF16)
    kvf = jnp.dot(ckvn, wkv_ref[...], preferred_element_type=F32)
    kr = kr_ref[...].astype(F32)
    kpe = (kr[:, :HEAD_DIM] * cos + kr[:, HEAD_DIM:] * sin).astype(k_ref.dtype)
    for h in range(C_HEADS):
        b = h * C_QW
        q_ref[:, h * C_QK:h * C_QK + C_NOPE] = qf[:, b:b + C_NOPE].astype(q_ref.dtype)
        rope = qf[:, b + HEAD_DIM:b + 2 * HEAD_DIM] * cos + qf[:, b + 2 * HEAD_DIM:b + 3 * HEAD_DIM] * sin
        q_ref[:, h * C_QK + C_NOPE:(h + 1) * C_QK] = rope.astype(q_ref.dtype)
        k_ref[:, h * C_QK:h * C_QK + C_NOPE] = kvf[:, h * C_NOPE:(h + 1) * C_NOPE].astype(k_ref.dtype)
        k_ref[:, h * C_QK + C_NOPE:(h + 1) * C_QK] = kpe
    v_ref[...] = kvf[:, C_HEADS * C_NOPE:].astype(v_ref.dtype)


def _rope_tables(s_len):
    half = C_ROPE // 2
    inv = ROPE_THETA ** (-jnp.arange(half, dtype=F32) / half)
    ang = jnp.arange(s_len, dtype=F32)[:, None] * inv[None, :]
    cos = jnp.cos(ang)
    sin = jnp.sin(ang)
    zeros = jnp.zeros((s_len, HEAD_DIM - C_ROPE), F32)
    return (jnp.concatenate([cos, cos, zeros], axis=-1),
            jnp.concatenate([-sin, sin, zeros], axis=-1))


def _mla_proj(h2, s_len, gq, gkv, wq, wkv):
    n = h2.shape[0]
    tm = min(MLA_TM, s_len)
    cos, sin = _rope_tables(s_len)
    per_seq = s_len // tm
    qk_shape = jax.ShapeDtypeStruct((n, C_HEADS * C_QK), BF16)
    cq_col = (D_IN - C_ROPE - C_KV_RANK - C_Q_RANK) // C_Q_RANK
    ckv_col = (D_IN - C_ROPE - C_KV_RANK) // C_KV_RANK
    kr_col = KR_OFF // (2 * HEAD_DIM)
    return pl.pallas_call(
        _mla_proj_kernel,
        grid=(n // tm,),
        in_specs=[pl.BlockSpec((tm, C_Q_RANK), lambda i: (i, cq_col)),
                  pl.BlockSpec((tm, C_KV_RANK), lambda i: (i, ckv_col)),
                  pl.BlockSpec((tm, 2 * HEAD_DIM), lambda i: (i, kr_col)),
                  pl.BlockSpec((1, C_Q_RANK), lambda i: (0, 0)),
                  pl.BlockSpec((1, C_KV_RANK), lambda i: (0, 0)),
                  pl.BlockSpec(wq.shape, lambda i: (0, 0)),
                  pl.BlockSpec(wkv.shape, lambda i: (0, 0)),
                  pl.BlockSpec((tm, HEAD_DIM), lambda i: (i % per_seq, 0)),
                  pl.BlockSpec((tm, HEAD_DIM), lambda i: (i % per_seq, 0))],
        out_specs=[pl.BlockSpec((tm, C_HEADS * C_QK), lambda i: (i, 0)),
                   pl.BlockSpec((tm, C_HEADS * C_QK), lambda i: (i, 0)),
                   pl.BlockSpec((tm, C_OUT), lambda i: (i, 0))],
        out_shape=[qk_shape, qk_shape, jax.ShapeDtypeStruct((n, C_OUT), BF16)],
        compiler_params=_cparams(("parallel",)),
        name="mla_proj",
    )(h2, h2, h2, gq.reshape(1, -1), gkv.reshape(1, -1), wq, wkv, cos, sin)


def _mla_attn_kernel(q_ref, k_ref, v_ref, g_ref, o_ref):
    scale = (C_NOPE + C_ROPE) ** -0.5
    outs = []
    ssq = jnp.zeros((q_ref.shape[0], 1), F32)
    for h in range(C_HEADS):
        qk = slice(h * C_QK, (h + 1) * C_QK)
        s = lax.dot_general(q_ref[:, qk], k_ref[:, qk], (((1,), (1,)), ((), ())),
                            preferred_element_type=F32) * scale
        m = jnp.max(s, axis=-1, keepdims=True)
        p = jnp.exp(s - m)
        p = p / jnp.sum(p, axis=-1, keepdims=True)
        o = jnp.dot(p.astype(BF16), v_ref[:, h * C_V:(h + 1) * C_V],
                    preferred_element_type=F32)
        ssq = ssq + jnp.sum(o * o, axis=-1, keepdims=True)
        outs.append(o)
    inv = lax.rsqrt(ssq / C_OUT + EPS)
    for h in range(C_HEADS):
        cols = slice(h * C_V, (h + 1) * C_V)
        o_ref[:, cols] = (outs[h] * inv * g_ref[:, cols]).astype(o_ref.dtype)


def _mla_attn(q3, k3, v3, gain):
    bsz, s_len, _ = q3.shape
    tq = min(MLA_TQ, s_len)
    return pl.pallas_call(
        _mla_attn_kernel,
        grid=(bsz, s_len // tq),
        in_specs=[pl.BlockSpec((None, tq, C_HEADS * C_QK), lambda b, i: (b, i, 0)),
                  pl.BlockSpec((None, s_len, C_HEADS * C_QK), lambda b, i: (b, 0, 0)),
                  pl.BlockSpec((None, s_len, C_OUT), lambda b, i: (b, 0, 0)),
                  pl.BlockSpec((1, C_OUT), lambda b, i: (0, 0))],
        out_specs=pl.BlockSpec((None, tq, C_OUT), lambda b, i: (b, i, 0)),
        out_shape=jax.ShapeDtypeStruct((bsz, s_len, C_OUT), BF16),
        compiler_params=_cparams(("parallel", "arbitrary")),
        name="mla_attn",
    )(q3, k3, v3, gain.reshape(1, C_OUT))


def _out_proj_kernel(x_ref, oa_ref, ob_ref, oc_ref, wa_ref, wb_ref, wc_ref, o_ref):
    acc = jnp.dot(oa_ref[...], wa_ref[...], preferred_element_type=F32)
    acc = acc + jnp.dot(ob_ref[...], wb_ref[...], preferred_element_type=F32)
    acc = acc + jnp.dot(oc_ref[...], wc_ref[...], preferred_element_type=F32)
    o_ref[...] = x_ref[...] + acc


def _out_proj(x, oa, ob, oc, w_o, tm=1024, tn=512):
    n, d = x.shape
    tm = min(tm, n)
    return pl.pallas_call(
        _out_proj_kernel,
        grid=(n // tm, d // tn),
        in_specs=[pl.BlockSpec((tm, tn), lambda i, j: (i, j)),
                  pl.BlockSpec((tm, A_Q), lambda i, j: (i, 0)),
                  pl.BlockSpec((tm, B_W), lambda i, j: (i, 0)),
                  pl.BlockSpec((tm, C_OUT), lambda i, j: (i, 0)),
                  pl.BlockSpec((A_Q, tn), lambda i, j: (0, j)),
                  pl.BlockSpec((B_W, tn), lambda i, j: (A_Q // B_W, j)),
                  pl.BlockSpec((C_OUT, tn), lambda i, j: ((A_Q + B_W) // C_OUT, j))],
        out_specs=pl.BlockSpec((tm, tn), lambda i, j: (i, j)),
        out_shape=jax.ShapeDtypeStruct((n, d), F32),
        compiler_params=_cparams(("parallel", "arbitrary")),
        name="out_proj",
    )(x, oa, ob, oc, w_o, w_o, w_o)


ROUTE_TM = 256
_CAND_ROWS = [(a, PEER_TOPK // (a + 1)) for a in range(PEER_TOPK)]
SUBLANES = 8


def _top_ranks(s, idx):
    rank = jnp.full(s.shape, float(PEER_KEYS), F32)
    vals = []
    for k in range(PEER_TOPK):
        m = jnp.max(s, axis=0, keepdims=True)
        first = jnp.min(jnp.where(s == m, idx, float(PEER_KEYS)), axis=0, keepdims=True)
        hit = idx == first
        rank = jnp.where(hit, float(k), rank)
        s = jnp.where(hit, NEG_INF, s)
        vals.append(m)
    return rank, vals


def _bf16_bits_high(x):
    return pltpu.bitcast(x.astype(BF16).astype(F32), jnp.uint32)


def _pack_halves(x):
    half = x.shape[0] // 2
    return (_bf16_bits_high(x[:half]) >> 16) | _bf16_bits_high(x[half:])


def _pack_twice(x):
    bits = _bf16_bits_high(x)
    return (bits >> 16) | bits


def _peer_route_kernel(x_ref, g_ref, wq_ref, keys_ref, xnt_ref, e2_ref, r2_ref, n1_ref, w1_ref):
    tm = x_ref.shape[0]
    xn = _rms(x_ref[...], g_ref[...])
    xnt = xn.T.astype(BF16)
    xnt_ref[...] = xnt
    qt = jnp.dot(wq_ref[...], xnt, preferred_element_type=F32)
    row = lax.broadcasted_iota(jnp.int32, (PEER_KEYS, tm), 0).astype(F32)
    row2 = jnp.where(row < PEER_KEYS // 2, 2.0 * row, 2.0 * row - (PEER_KEYS - 1))

    def route_head(h):
        sc = []
        for c in range(2):
            row0 = (h * 2 + c) * PEER_HALF
            sc.append(jnp.dot(keys_ref[c], qt[row0:row0 + PEER_HALF, :],
                              preferred_element_type=F32,
                              precision=lax.Precision.HIGHEST))
        rank1, t1 = _top_ranks(sc[0], row)
        rank2, t2 = _top_ranks(sc[1], row2)
        e1 = [jnp.exp(t1[a] - t1[0]) for a in range(PEER_TOPK)]
        e2 = [jnp.exp(t2[b] - t2[0]) for b in range(PEER_TOPK)]
        groups, gmeta = [], []
        for a, nb in _CAND_ROWS:
            for b0 in range(0, nb, SUBLANES):
                bs = list(range(b0, min(b0 + SUBLANES, nb)))
                rows = [t1[a] + t2[b] for b in bs]
                rows += [jnp.full((1, tm), NEG_INF, F32)] * (SUBLANES - len(bs))
                groups.append(jnp.concatenate(rows, axis=0))
                gmeta.append((a, bs))
        cand = jnp.concatenate(groups, axis=0)
        cidx = lax.broadcasted_iota(jnp.int32, cand.shape, 0).astype(F32)
        crank, _ = _top_ranks(cand, cidx)
        self_f = jnp.where(crank < float(PEER_TOPK), 1.0, 0.0)
        n_a = [jnp.zeros((1, tm), F32) for _ in range(PEER_TOPK)]
        z = jnp.zeros((1, tm), F32)
        for gi, (a, bs) in enumerate(gmeta):
            blk = self_f[gi * SUBLANES:(gi + 1) * SUBLANES, :]
            n_a[a] = n_a[a] + jnp.sum(blk, axis=0, keepdims=True)
            for bi, b in enumerate(bs):
                z = z + blk[bi:bi + 1, :] * (e1[a] * e2[b])
        n1 = jnp.zeros(rank1.shape, F32)
        for a in range(PEER_TOPK):
            n1 = jnp.where(rank1 == float(a), n_a[a], n1)
        e2_ref[h] = _pack_halves(jnp.exp(sc[1] - t2[0]))
        r2_ref[h] = _pack_halves(rank2)
        n1_ref[h] = _pack_twice(n1).reshape(PEER_KEYS // SUBLANES, SUBLANES, tm)
        w1_ref[h] = _pack_twice(jnp.exp(sc[0] - t1[0]) / z).reshape(PEER_KEYS // SUBLANES, SUBLANES, tm)

    for h in range(PEER_HEADS):
        route_head(h)


def _peer_route(x, g, wq_t, sub_keys):
    n, d = x.shape
    tm = min(ROUTE_TM, n)
    plane = jax.ShapeDtypeStruct((PEER_HEADS, PEER_KEYS // 2, n), jnp.uint32)
    plane_spec = pl.BlockSpec((PEER_HEADS, PEER_KEYS // 2, tm), lambda i: (0, 0, i))
    table = jax.ShapeDtypeStruct((PEER_HEADS, PEER_KEYS // SUBLANES, SUBLANES, n), jnp.uint32)
    table_spec = pl.BlockSpec((PEER_HEADS, PEER_KEYS // SUBLANES, SUBLANES, tm),
                              lambda i: (0, 0, 0, i))
    return pl.pallas_call(
        _peer_route_kernel,
        grid=(n // tm,),
        in_specs=[pl.BlockSpec((tm, d), lambda i: (i, 0)),
                  pl.BlockSpec((1, d), lambda i: (0, 0)),
                  pl.BlockSpec(wq_t.shape, lambda i: (0, 0)),
                  pl.BlockSpec(sub_keys.shape, lambda i: (0, 0, 0))],
        out_specs=[pl.BlockSpec((d, tm), lambda i: (0, i)),
                   plane_spec, plane_spec, table_spec, table_spec],
        out_shape=[jax.ShapeDtypeStruct((d, n), BF16), plane, plane, table, table],
        compiler_params=_cparams(("parallel",)),
        name="peer_route",
    )(x, g.reshape(1, d), wq_t, sub_keys)


PEER_T = 512
PEER_ET = SUBLANES * PEER_KEYS
LANES = 128


def _rows_bf16(words):
    return pltpu.bitcast(jnp.broadcast_to(words, (PEER_KEYS // 2, LANES)), BF16)


def _peer_dense_kernel(u_ref, vt_ref, xnt_ref, e2_ref, r2_ref, n1_ref, w1_ref, yt_ref, h_ref):
    e = pl.program_id(1)
    t = xnt_ref.shape[1]

    @pl.when(e == 0)
    def _():
        yt_ref[...] = jnp.zeros_like(yt_ref)

    at = jnp.dot(u_ref[...], xnt_ref[...], preferred_element_type=F32)
    for rl in range(SUBLANES):
        for tc in range(t // LANES):
            cols = slice(tc * LANES, (tc + 1) * LANES)
            gate = jnp.zeros((PEER_KEYS, LANES), BF16)
            for h in range(PEER_HEADS):
                n1r = _rows_bf16(n1_ref[h, e, rl:rl + 1, cols])
                w1r = _rows_bf16(w1_ref[h, e, rl:rl + 1, cols])
                r2 = pltpu.bitcast(r2_ref[h, :, cols], BF16)
                e2 = pltpu.bitcast(e2_ref[h, :, cols], BF16)
                gate = gate + jnp.where(r2 < n1r, e2 * w1r, jnp.zeros((), BF16))
            a = at[rl * PEER_KEYS:(rl + 1) * PEER_KEYS, cols]
            h_ref[rl * PEER_KEYS:(rl + 1) * PEER_KEYS, cols] = jax.nn.gelu(a.astype(BF16)) * gate
    yt_ref[...] += jnp.dot(vt_ref[...], h_ref[...], preferred_element_type=F32)


def _peer_dense(u, vt, xnt, e2, r2, n1, w1):
    n_exp, d = u.shape
    n = xnt.shape[1]
    t = min(PEER_T, n)
    plane_spec = pl.BlockSpec((PEER_HEADS, PEER_KEYS // 2, t), lambda i, e: (0, 0, i))
    table_spec = pl.BlockSpec((PEER_HEADS, PEER_KEYS // SUBLANES, SUBLANES, t),
                              lambda i, e: (0, 0, 0, i))
    return pl.pallas_call(
        _peer_dense_kernel,
        grid=(n // t, n_exp // PEER_ET),
        in_specs=[pl.BlockSpec((PEER_ET, d), lambda i, e: (e, 0)),
                  pl.BlockSpec((d, PEER_ET), lambda i, e: (0, e)),
                  pl.BlockSpec((d, t), lambda i, e: (0, i)),
                  plane_spec, plane_spec, table_spec, table_spec],
        out_specs=pl.BlockSpec((d, t), lambda i, e: (0, i)),
        out_shape=jax.ShapeDtypeStruct((d, n), F32),
        scratch_shapes=[pltpu.VMEM((PEER_ET, t), BF16)],
        compiler_params=_cparams(("parallel", "arbitrary")),
        name="peer_dense",
    )(u, vt, xnt, e2, r2, n1, w1)


def _add_t_kernel(x_ref, yt_ref, o_ref):
    o_ref[...] = x_ref[...] + yt_ref[...].T


def _add_t_norm_kernel(x_ref, yt_ref, g_ref, o_ref):
    o_ref[...] = _rms(x_ref[...] + yt_ref[...].T, g_ref[...])


def _add_transposed(x, yt, gain=None, tm=512):
    n, d = x.shape
    tm = min(tm, n)
    in_specs = [pl.BlockSpec((tm, d), lambda i: (i, 0)),
                pl.BlockSpec((d, tm), lambda i: (0, i))]
    args = [x, yt]
    body = _add_t_kernel
    if gain is not None:
        in_specs.append(pl.BlockSpec((1, d), lambda i: (0, 0)))
        args.append(gain.reshape(1, d))
        body = _add_t_norm_kernel
    return pl.pallas_call(
        body,
        grid=(n // tm,),
        in_specs=in_specs,
        out_specs=pl.BlockSpec((tm, d), lambda i: (i, 0)),
        out_shape=jax.ShapeDtypeStruct((n, d), F32),
        compiler_params=_cparams(("parallel",)),
        name="add_transposed",
    )(*args)


def _swap_halves(w):
    half = w.shape[-1] // 2
    return jnp.concatenate([w[..., half:], w[..., :half]], axis=-1)


def _prep_w_in(w_in):
    d = w_in.shape[0]
    kr = w_in[:, KR_OFF:D_IN]
    z = jnp.zeros((d, HEAD_DIM - C_ROPE), w_in.dtype)
    return jnp.concatenate([w_in[:, :KR_OFF], kr, z, _swap_halves(kr), z], axis=-1).astype(BF16)


def _prep_w_uq(w_uq):
    r = w_uq.shape[0]
    z = jnp.zeros((r, HEAD_DIM - C_ROPE), w_uq.dtype)
    blocks = []
    for h in range(C_HEADS):
        b = h * (C_NOPE + C_ROPE)
        rope = w_uq[:, b + C_NOPE:b + C_NOPE + C_ROPE]
        blocks += [w_uq[:, b:b + C_NOPE], rope, z, _swap_halves(rope), z]
    return jnp.concatenate(blocks, axis=-1).astype(BF16)


def _prep_w_ukv(w_ukv):
    ks = [w_ukv[:, h * (C_NOPE + C_V):h * (C_NOPE + C_V) + C_NOPE] for h in range(C_HEADS)]
    vs = [w_ukv[:, h * (C_NOPE + C_V) + C_NOPE:(h + 1) * (C_NOPE + C_V)] for h in range(C_HEADS)]
    return jnp.concatenate(ks + vs, axis=-1).astype(BF16)


def _mixer(x2, bsz, s_len, ln1, w_in, a_sink, b_rel_bias, c_q_norm, c_kv_norm, c_w_uq, c_w_ukv,
           out_norm, w_o):
    h2 = _norm_matmul(x2, ln1, _prep_w_in(w_in))
    h3 = h2.reshape(bsz, s_len, D_IN_PAD)
    oa = _attn_a(h3, a_sink.astype(F32), out_norm[:A_Q])
    ob = _attn_b(h3, b_rel_bias, out_norm[A_Q:A_Q + B_W])
    qc, kc, vc = _mla_proj(h2, s_len, c_q_norm, c_kv_norm, _prep_w_uq(c_w_uq), _prep_w_ukv(c_w_ukv))
    oc = _mla_attn(qc.reshape(bsz, s_len, -1), kc.reshape(bsz, s_len, -1),
                   vc.reshape(bsz, s_len, -1), out_norm[A_Q + B_W:])
    n = bsz * s_len
    return _out_proj(x2, oa.reshape(n, A_Q), ob.reshape(n, B_W), oc.reshape(n, C_OUT),
                     w_o.astype(BF16))


def _peer(x2, ln2, w_q, sub_keys, u, vv):
    keys = sub_keys.astype(F32)
    keys = jnp.stack([keys[0], jnp.concatenate([keys[1, 0::2], keys[1, 1::2]], axis=0)])
    xnt, e2, r2, n1, w1 = _peer_route(x2, ln2, w_q.T.astype(BF16), keys)
    return _peer_dense(u.astype(BF16), vv.T.astype(BF16), xnt, e2, r2, n1, w1)


def kernel(x, ln1, w_in, a_sink, b_rel_bias, c_q_norm, c_kv_norm, c_w_uq, c_w_ukv, out_norm, w_o,
           ln2, peer_w_q, peer_sub_keys, peer_u, peer_v, final_norm):
    bsz, s_len, d = x.shape
    depth = ln1.shape[0]
    x2 = x.reshape(bsz * s_len, d)
    for l in range(depth):
        x2 = _mixer(x2, bsz, s_len, ln1[l], w_in[l], a_sink[l], b_rel_bias[l], c_q_norm[l],
                    c_kv_norm[l], c_w_uq[l], c_w_ukv[l], out_norm[l], w_o[l])
        yt = _peer(x2, ln2[l], peer_w_q[l], peer_sub_keys[l], peer_u[l], peer_v[l])
        x2 = _add_transposed(x2, yt, final_norm if l == depth - 1 else None)
    return x2.reshape(bsz, s_len, d)
```

```python
import functools
import math

import jax
import jax.numpy as jnp
import numpy as np
from jax import lax
from jax.experimental import pallas as pl
from jax.experimental.pallas import tpu as pltpu

F32 = jnp.float32
BF16 = jnp.bfloat16

EPS = 1e-6
HEAD_DIM = 128
A_HEADS = 8
A_KV_HEADS = 2
A_GROUP = A_HEADS // A_KV_HEADS
A_WINDOW = 128
B_HEADS = 4
GRID_W = 64
NB_ROWS = 8
NB_COLS = 16
C_HEADS = 4
C_Q_RANK = 512
C_KV_RANK = 256
C_NOPE = 128
C_ROPE = 64
C_V = 128
ROPE_THETA = 10000.0
A_Q = A_HEADS * HEAD_DIM
A_KV = A_KV_HEADS * HEAD_DIM
B_W = B_HEADS * HEAD_DIM
C_OUT = C_HEADS * C_V
D_MIX = A_Q + B_W + C_OUT
D_IN = A_Q + 2 * A_KV + 3 * B_W + C_Q_RANK + C_KV_RANK + C_ROPE
D_IN_PAD = 4096
KR_OFF = D_IN - C_ROPE
PEER_HEADS = 8
PEER_KEYS = 128
PEER_HALF = 64
PEER_TOPK = 16

VMEM_LIMIT = 56 * 1024 * 1024
NEG_INF = float("-inf")


def _cparams(sem):
    return pltpu.CompilerParams(dimension_semantics=sem, vmem_limit_bytes=VMEM_LIMIT)


def _rms(x, g):
    ms = jnp.mean(x * x, axis=-1, keepdims=True)
    return x * lax.rsqrt(ms + EPS) * g


def _norm_matmul_kernel(x_ref, g_ref, w_ref, o_ref, xn_ref):
    @pl.when(pl.program_id(1) == 0)
    def _():
        xn_ref[...] = _rms(x_ref[...], g_ref[...]).astype(BF16)

    o_ref[...] = jnp.dot(xn_ref[...], w_ref[...],
                         preferred_element_type=F32).astype(o_ref.dtype)


def _norm_matmul(x, g, w, tm=1024, tn=512):
    n, d = x.shape
    nout = w.shape[1]
    tm = min(tm, n)
    return pl.pallas_call(
        _norm_matmul_kernel,
        grid=(n // tm, nout // tn),
        in_specs=[pl.BlockSpec((tm, d), lambda i, j: (i, 0)),
                  pl.BlockSpec((1, d), lambda i, j: (0, 0)),
                  pl.BlockSpec((d, tn), lambda i, j: (0, j))],
        out_specs=pl.BlockSpec((tm, tn), lambda i, j: (i, j)),
        out_shape=jax.ShapeDtypeStruct((n, nout), BF16),
        scratch_shapes=[pltpu.VMEM((tm, d), BF16)],
        compiler_params=_cparams(("parallel", "arbitrary")),
        name="norm_matmul",
    )(x, g.reshape(1, d), w)


A_TQ = 512
A_BAND = 3 * A_WINDOW


def _attn_a_kernel(sink_ref, q_ref, k_ref, v_ref, g_ref, o_ref):
    i = pl.program_id(1)
    s_len = k_ref.shape[0]
    scale = HEAD_DIM ** -0.5
    slopes = [2.0 ** (-8.0 * (h + 1) / A_HEADS) for h in range(A_HEADS)]
    for j in range(A_TQ // A_WINDOW):
        q0 = i * A_TQ + j * A_WINDOW
        start = pl.multiple_of(jnp.clip(q0 - A_WINDOW, 0, s_len - A_BAND), A_WINDOW)
        kb = k_ref[pl.ds(start, A_BAND), :]
        vb = v_ref[pl.ds(start, A_BAND), :]
        q_pos = q0 + lax.broadcasted_iota(jnp.int32, (A_WINDOW, A_BAND), 0)
        k_pos = start + lax.broadcasted_iota(jnp.int32, (A_WINDOW, A_BAND), 1)
        dist = jnp.abs(q_pos - k_pos)
        valid = dist <= A_WINDOW
        dist_f = dist.astype(F32)
        outs = []
        ssq = jnp.zeros((A_WINDOW, 1), F32)
        for kh in range(A_KV_HEADS):
            kk = kb[:, kh * HEAD_DIM:(kh + 1) * HEAD_DIM]
            vv = vb[:, kh * HEAD_DIM:(kh + 1) * HEAD_DIM]
            for g in range(A_GROUP):
                h = kh * A_GROUP + g
                qh = q_ref[j * A_WINDOW:(j + 1) * A_WINDOW, h * HEAD_DIM:(h + 1) * HEAD_DIM]
                s = lax.dot_general(qh, kk, (((1,), (1,)), ((), ())),
                                    preferred_element_type=F32) * scale
                s = s - slopes[h] * dist_f
                s = jnp.where(valid, s, NEG_INF)
                sink = sink_ref[h]
                m = jnp.maximum(jnp.max(s, axis=-1, keepdims=True), sink)
                p = jnp.exp(s - m)
                denom = jnp.sum(p, axis=-1, keepdims=True) + jnp.exp(sink - m)
                o = jnp.dot(p.astype(BF16), vv, preferred_element_type=F32) * (1.0 / denom)
                ssq = ssq + jnp.sum(o * o, axis=-1, keepdims=True)
                outs.append(o)
        inv = lax.rsqrt(ssq / A_Q + EPS)
        for h in range(A_HEADS):
            cols = slice(h * HEAD_DIM, (h + 1) * HEAD_DIM)
            o_ref[j * A_WINDOW:(j + 1) * A_WINDOW, cols] = (
                outs[h] * inv * g_ref[:, cols]).astype(o_ref.dtype)


def _attn_a(h3, sink, gain):
    bsz, s_len, _ = h3.shape
    assert s_len % A_TQ == 0 and s_len >= A_BAND
    kcol = A_Q // A_KV
    return pl.pallas_call(
        _attn_a_kernel,
        grid=(bsz, s_len // A_TQ),
        in_specs=[pl.BlockSpec(memory_space=pltpu.SMEM),
                  pl.BlockSpec((None, A_TQ, A_Q), lambda b, i: (b, i, 0)),
                  pl.BlockSpec((None, s_len, A_KV), lambda b, i: (b, 0, kcol)),
                  pl.BlockSpec((None, s_len, A_KV), lambda b, i: (b, 0, kcol + 1)),
                  pl.BlockSpec((1, A_Q), lambda b, i: (0, 0))],
        out_specs=pl.BlockSpec((None, A_TQ, A_Q), lambda b, i: (b, i, 0)),
        out_shape=jax.ShapeDtypeStruct((bsz, s_len, A_Q), BF16),
        compiler_params=_cparams(("parallel", "arbitrary")),
        name="attn_a",
    )(sink, h3, h3, h3, gain.reshape(1, A_Q))


B_RQ = 8
B_KEYS = NB_ROWS * GRID_W
B_QCOL = (A_Q + 2 * A_KV) // B_W


def _attn_b_kernel(q_ref, k_ref, v_ref, bias_ref, g_ref, o_ref):
    i = pl.program_id(1)
    rows = k_ref.shape[0] // GRID_W
    scale = HEAD_DIM ** -0.5
    for jr in range(B_RQ):
        r = i * B_RQ + jr
        r_start = jnp.clip(r - NB_ROWS // 2, 0, rows - NB_ROWS)
        off = r_start - r + NB_ROWS - 1
        kstart = pl.multiple_of(r_start * GRID_W, GRID_W)
        kb = k_ref[pl.ds(kstart, B_KEYS), :]
        vb = v_ref[pl.ds(kstart, B_KEYS), :]
        hs = [slice(h * HEAD_DIM, (h + 1) * HEAD_DIM) for h in range(B_HEADS)]
        q3 = jnp.stack([q_ref[jr * GRID_W:(jr + 1) * GRID_W, c] for c in hs])
        k3 = jnp.stack([kb[:, c] for c in hs])
        v3 = jnp.stack([vb[:, c] for c in hs])
        s = jnp.einsum('hqd,hkd->hqk', q3, k3, preferred_element_type=F32) * scale + bias_ref[off]
        m = jnp.max(s, axis=-1, keepdims=True)
        p = jnp.exp(s - m)
        inv_l = 1.0 / jnp.sum(p, axis=-1, keepdims=True)
        o = jnp.einsum('hqk,hkd->hqd', p.astype(BF16), v3, preferred_element_type=F32) * inv_l
        ssq = jnp.sum(jnp.sum(o * o, axis=-1, keepdims=True), axis=0)
        outs = [o[h] for h in range(B_HEADS)]
        inv = lax.rsqrt(ssq / B_W + EPS)
        for h in range(B_HEADS):
            cols = slice(h * HEAD_DIM, (h + 1) * HEAD_DIM)
            o_ref[jr * GRID_W:(jr + 1) * GRID_W, cols] = (
                outs[h] * inv * g_ref[:, cols]).astype(o_ref.dtype)


def _nb_bias_table(rel_bias):
    col = jnp.arange(GRID_W)
    c_start = jnp.clip(col - NB_COLS // 2, 0, GRID_W - NB_COLS)
    col_ok = (col[None, :] >= c_start[:, None]) & (col[None, :] < c_start[:, None] + NB_COLS)
    dc_idx = jnp.clip(col[None, :] - col[:, None] + NB_COLS - 1, 0, 2 * NB_COLS - 2)
    rb = rel_bias.astype(F32)[:, :, dc_idx]
    rb = jnp.where(col_ok[None, None], rb, NEG_INF)
    tabs = []
    for off in range(NB_ROWS):
        t = jnp.transpose(rb[:, off:off + NB_ROWS], (0, 2, 1, 3))
        tabs.append(t.reshape(B_HEADS, GRID_W, B_KEYS))
    return jnp.stack(tabs)


def _attn_b(h3, rel_bias, gain):
    bsz, s_len, _ = h3.shape
    rows = s_len // GRID_W
    assert rows >= NB_ROWS and rows % B_RQ == 0
    tq = B_RQ * GRID_W
    bias = _nb_bias_table(rel_bias)
    return pl.pallas_call(
        _attn_b_kernel,
        grid=(bsz, rows // B_RQ),
        in_specs=[pl.BlockSpec((None, tq, B_W), lambda b, i: (b, i, B_QCOL)),
                  pl.BlockSpec((None, s_len, B_W), lambda b, i: (b, 0, B_QCOL + 1)),
                  pl.BlockSpec((None, s_len, B_W), lambda b, i: (b, 0, B_QCOL + 2)),
                  pl.BlockSpec((NB_ROWS, B_HEADS, GRID_W, B_KEYS), lambda b, i: (0, 0, 0, 0)),
                  pl.BlockSpec((1, B_W), lambda b, i: (0, 0))],
        out_specs=pl.BlockSpec((None, tq, B_W), lambda b, i: (b, i, 0)),
        out_shape=jax.ShapeDtypeStruct((bsz, s_len, B_W), BF16),
        compiler_params=_cparams(("parallel", "arbitrary")),
        name="attn_b",
    )(h3, h3, h3, bias, gain.reshape(1, B_W))


C_QK = 256
C_QW = 3 * HEAD_DIM
MLA_TM = 512
MLA_TQ = 256


def _mla_proj_kernel(cq_ref, ckv_ref, kr_ref, gq_ref, gkv_ref, wq_ref, wkv_ref,
                     cos_ref, sin_ref, q_ref, k_ref, v_ref):
    cos = cos_ref[...]
    sin = sin_ref[...]
    cqn = _rms(cq_ref[...].astype(F32), gq_ref[...]).astype(BF16)
    qf = jnp.dot(cqn, wq_ref[...], preferred_element_type=F32) * ((C_NOPE + C_ROPE) ** -0.5)
    ckvn = _rms(ckv_ref[...].astype(F32), gkv_ref[...]).astype(BF16)
    kvf = jnp.dot(ckvn, wkv_ref[...], preferred_element_type=F32)
    kr = kr_ref[...].astype(F32)
    kpe = (kr[:, :HEAD_DIM] * cos + kr[:, HEAD_DIM:] * sin).astype(k_ref.dtype)
    for h in range(C_HEADS):
        b = h * C_QW
        q_ref[:, h * C_QK:h * C_QK + C_NOPE] = qf[:, b:b + C_NOPE].astype(q_ref.dtype)
        rope = qf[:, b + HEAD_DIM:b + 2 * HEAD_DIM] * cos + qf[:, b + 2 * HEAD_DIM:b + 3 * HEAD_DIM] * sin
        q_ref[:, h * C_QK + C_NOPE:(h + 1) * C_QK] = rope.astype(q_ref.dtype)
        k_ref[:, h * C_QK:h * C_QK + C_NOPE] = kvf[:, h * C_NOPE:(h + 1) * C_NOPE].astype(k_ref.dtype)
        k_ref[:, h * C_QK + C_NOPE:(h + 1) * C_QK] = kpe
    v_ref[...] = kvf[:, C_HEADS * C_NOPE:].astype(v_ref.dtype)


def _rope_tables(s_len):
    half = C_ROPE // 2
    inv = ROPE_THETA ** (-jnp.arange(half, dtype=F32) / half)
    ang = jnp.arange(s_len, dtype=F32)[:, None] * inv[None, :]
    cos = jnp.cos(ang)
    sin = jnp.sin(ang)
    zeros = jnp.zeros((s_len, HEAD_DIM - C_ROPE), F32)
    return (jnp.concatenate([cos, cos, zeros], axis=-1),
            jnp.concatenate([-sin, sin, zeros], axis=-1))


def _mla_proj(h2, s_len, gq, gkv, wq, wkv):
    n = h2.shape[0]
    tm = min(MLA_TM, s_len)
    cos, sin = _rope_tables(s_len)
    per_seq = s_len // tm
    qk_shape = jax.ShapeDtypeStruct((n, C_HEADS * C_QK), BF16)
    cq_col = (D_IN - C_ROPE - C_KV_RANK - C_Q_RANK) // C_Q_RANK
    ckv_col = (D_IN - C_ROPE - C_KV_RANK) // C_KV_RANK
    kr_col = KR_OFF // (2 * HEAD_DIM)
    return pl.pallas_call(
        _mla_proj_kernel,
        grid=(n // tm,),
        in_specs=[pl.BlockSpec((tm, C_Q_RANK), lambda i: (i, cq_col)),
                  pl.BlockSpec((tm, C_KV_RANK), lambda i: (i, ckv_col)),
                  pl.BlockSpec((tm, 2 * HEAD_DIM), lambda i: (i, kr_col)),
                  pl.BlockSpec((1, C_Q_RANK), lambda i: (0, 0)),
                  pl.BlockSpec((1, C_KV_RANK), lambda i: (0, 0)),
                  pl.BlockSpec(wq.shape, lambda i: (0, 0)),
                  pl.BlockSpec(wkv.shape, lambda i: (0, 0)),
                  pl.BlockSpec((tm, HEAD_DIM), lambda i: (i % per_seq, 0)),
                  pl.BlockSpec((tm, HEAD_DIM), lambda i: (i % per_seq, 0))],
        out_specs=[pl.BlockSpec((tm, C_HEADS * C_QK), lambda i: (i, 0)),
                   pl.BlockSpec((tm, C_HEADS * C_QK), lambda i: (i, 0)),
                   pl.BlockSpec((tm, C_OUT), lambda i: (i, 0))],
        out_shape=[qk_shape, qk_shape, jax.ShapeDtypeStruct((n, C_OUT), BF16)],
        compiler_params=_cparams(("parallel",)),
        name="mla_proj",
    )(h2, h2, h2, gq.reshape(1, -1), gkv.reshape(1, -1), wq, wkv, cos, sin)


def _mla_attn_kernel(q_ref, k_ref, v_ref, g_ref, o_ref):
    outs = []
    ssq = jnp.zeros((q_ref.shape[0], 1), F32)
    for h in range(C_HEADS):
        qk = slice(h * C_QK, (h + 1) * C_QK)
        s = lax.dot_general(q_ref[:, qk], k_ref[:, qk], (((1,), (1,)), ((), ())),
                            preferred_element_type=F32)
        m = jnp.max(s, axis=-1, keepdims=True)
        p = jnp.exp(s - m)
        inv_l = 1.0 / jnp.sum(p, axis=-1, keepdims=True)
        o = jnp.dot(p.astype(BF16), v_ref[:, h * C_V:(h + 1) * C_V],
                    preferred_element_type=F32) * inv_l
        ssq = ssq + jnp.sum(o * o, axis=-1, keepdims=True)
        outs.append(o)
    inv = lax.rsqrt(ssq / C_OUT + EPS)
    for h in range(C_HEADS):
        cols = slice(h * C_V, (h + 1) * C_V)
        o_ref[:, cols] = (outs[h] * inv * g_ref[:, cols]).astype(o_ref.dtype)


def _mla_attn(q3, k3, v3, gain):
    bsz, s_len, _ = q3.shape
    tq = min(MLA_TQ, s_len)
    return pl.pallas_call(
        _mla_attn_kernel,
        grid=(bsz, s_len // tq),
        in_specs=[pl.BlockSpec((None, tq, C_HEADS * C_QK), lambda b, i: (b, i, 0)),
                  pl.BlockSpec((None, s_len, C_HEADS * C_QK), lambda b, i: (b, 0, 0)),
                  pl.BlockSpec((None, s_len, C_OUT), lambda b, i: (b, 0, 0)),
                  pl.BlockSpec((1, C_OUT), lambda b, i: (0, 0))],
        out_specs=pl.BlockSpec((None, tq, C_OUT), lambda b, i: (b, i, 0)),
        out_shape=jax.ShapeDtypeStruct((bsz, s_len, C_OUT), BF16),
        compiler_params=_cparams(("parallel", "arbitrary")),
        name="mla_attn",
    )(q3, k3, v3, gain.reshape(1, C_OUT))


def _out_proj_kernel(x_ref, oa_ref, ob_ref, oc_ref, wa_ref, wb_ref, wc_ref, o_ref):
    acc = jnp.dot(oa_ref[...], wa_ref[...], preferred_element_type=F32)
    acc = acc + jnp.dot(ob_ref[...], wb_ref[...], preferred_element_type=F32)
    acc = acc + jnp.dot(oc_ref[...], wc_ref[...], preferred_element_type=F32)
    o_ref[...] = x_ref[...] + acc


def _out_proj(x, oa, ob, oc, w_o, tm=1024, tn=512):
    n, d = x.shape
    tm = min(tm, n)
    return pl.pallas_call(
        _out_proj_kernel,
        grid=(n // tm, d // tn),
        in_specs=[pl.BlockSpec((tm, tn), lambda i, j: (i, j)),
                  pl.BlockSpec((tm, A_Q), lambda i, j: (i, 0)),
                  pl.BlockSpec((tm, B_W), lambda i, j: (i, 0)),
                  pl.BlockSpec((tm, C_OUT), lambda i, j: (i, 0)),
                  pl.BlockSpec((A_Q, tn), lambda i, j: (0, j)),
                  pl.BlockSpec((B_W, tn), lambda i, j: (A_Q // B_W, j)),
                  pl.BlockSpec((C_OUT, tn), lambda i, j: ((A_Q + B_W) // C_OUT, j))],
        out_specs=pl.BlockSpec((tm, tn), lambda i, j: (i, j)),
        out_shape=jax.ShapeDtypeStruct((n, d), F32),
        compiler_params=_cparams(("parallel", "arbitrary")),
        name="out_proj",
    )(x, oa, ob, oc, w_o, w_o, w_o)


ROUTE_TM = 256
_CAND_ROWS = [(a, PEER_TOPK // (a + 1)) for a in range(PEER_TOPK)]
SUBLANES = 8


def _top_ranks(s, idx):
    rank = jnp.full(s.shape, float(PEER_KEYS), F32)
    vals = []
    for k in range(PEER_TOPK):
        m = jnp.max(s, axis=0, keepdims=True)
        first = jnp.min(jnp.where(s == m, idx, float(PEER_KEYS)), axis=0, keepdims=True)
        hit = idx == first
        rank = jnp.where(hit, float(k), rank)
        s = jnp.where(hit, NEG_INF, s)
        vals.append(m)
    return rank, vals


def _bf16_bits_high(x):
    return pltpu.bitcast(x.astype(BF16).astype(F32), jnp.uint32)


def _pack_halves(x):
    half = x.shape[0] // 2
    return (_bf16_bits_high(x[:half]) >> 16) | _bf16_bits_high(x[half:])


def _pack_twice(x):
    bits = _bf16_bits_high(x)
    return (bits >> 16) | bits


def _peer_route_kernel(x_ref, g_ref, wq_ref, keys_ref, xnt_ref, e2_ref, r2_ref, n1_ref, w1_ref):
    tm = x_ref.shape[0]
    xn = _rms(x_ref[...], g_ref[...])
    xnt = xn.T.astype(BF16)
    xnt_ref[...] = xnt
    qt = jnp.dot(wq_ref[...], xnt, preferred_element_type=F32)
    row = lax.broadcasted_iota(jnp.int32, (PEER_KEYS, tm), 0).astype(F32)
    row2 = jnp.where(row < PEER_KEYS // 2, 2.0 * row, 2.0 * row - (PEER_KEYS - 1))

    def route_head(h):
        sc = []
        for c in range(2):
            row0 = (h * 2 + c) * PEER_HALF
            sc.append(jnp.dot(keys_ref[c], qt[row0:row0 + PEER_HALF, :],
                              preferred_element_type=F32,
                              precision=lax.Precision.HIGHEST))
        rank1, t1 = _top_ranks(sc[0], row)
        rank2, t2 = _top_ranks(sc[1], row2)
        e1 = [jnp.exp(t1[a] - t1[0]) for a in range(PEER_TOPK)]
        e2 = [jnp.exp(t2[b] - t2[0]) for b in range(PEER_TOPK)]
        groups, gmeta = [], []
        for a, nb in _CAND_ROWS:
            for b0 in range(0, nb, SUBLANES):
                bs = list(range(b0, min(b0 + SUBLANES, nb)))
                rows = [t1[a] + t2[b] for b in bs]
                rows += [jnp.full((1, tm), NEG_INF, F32)] * (SUBLANES - len(bs))
                groups.append(jnp.concatenate(rows, axis=0))
                gmeta.append((a, bs))
        cand = jnp.concatenate(groups, axis=0)
        cidx = lax.broadcasted_iota(jnp.int32, cand.shape, 0).astype(F32)
        crank, _ = _top_ranks(cand, cidx)
        self_f = jnp.where(crank < float(PEER_TOPK), 1.0, 0.0)
        n_a = [jnp.zeros((1, tm), F32) for _ in range(PEER_TOPK)]
        z = jnp.zeros((1, tm), F32)
        for gi, (a, bs) in enumerate(gmeta):
            blk = self_f[gi * SUBLANES:(gi + 1) * SUBLANES, :]
            n_a[a] = n_a[a] + jnp.sum(blk, axis=0, keepdims=True)
            for bi, b in enumerate(bs):
                z = z + blk[bi:bi + 1, :] * (e1[a] * e2[b])
        n1 = jnp.zeros(rank1.shape, F32)
        for a in range(PEER_TOPK):
            n1 = jnp.where(rank1 == float(a), n_a[a], n1)
        e2_ref[h] = _pack_halves(jnp.exp(sc[1] - t2[0]))
        r2_ref[h] = _pack_halves(rank2)
        n1_ref[h] = _pack_twice(n1).reshape(PEER_KEYS // SUBLANES, SUBLANES, tm)
        w1_ref[h] = _pack_twice(jnp.exp(sc[0] - t1[0]) / z).reshape(PEER_KEYS // SUBLANES, SUBLANES, tm)

    for h in range(PEER_HEADS):
        route_head(h)


def _peer_route(x, g, wq_t, sub_keys):
    n, d = x.shape
    tm = min(ROUTE_TM, n)
    plane = jax.ShapeDtypeStruct((PEER_HEADS, PEER_KEYS // 2, n), jnp.uint32)
    plane_spec = pl.BlockSpec((PEER_HEADS, PEER_KEYS // 2, tm), lambda i: (0, 0, i))
    table = jax.ShapeDtypeStruct((PEER_HEADS, PEER_KEYS // SUBLANES, SUBLANES, n), jnp.uint32)
    table_spec = pl.BlockSpec((PEER_HEADS, PEER_KEYS // SUBLANES, SUBLANES, tm),
                              lambda i: (0, 0, 0, i))
    return pl.pallas_call(
        _peer_route_kernel,
        grid=(n // tm,),
        in_specs=[pl.BlockSpec((tm, d), lambda i: (i, 0)),
                  pl.BlockSpec((1, d), lambda i: (0, 0)),
                  pl.BlockSpec(wq_t.shape, lambda i: (0, 0)),
                  pl.BlockSpec(sub_keys.shape, lambda i: (0, 0, 0))],
        out_specs=[pl.BlockSpec((d, tm), lambda i: (0, i)),
                   plane_spec, plane_spec, table_spec, table_spec],
        out_shape=[jax.ShapeDtypeStruct((d, n), BF16), plane, plane, table, table],
        compiler_params=_cparams(("parallel",)),
        name="peer_route",
    )(x, g.reshape(1, d), wq_t, sub_keys)


PEER_T = 512
PEER_ET = SUBLANES * PEER_KEYS
LANES = 128


def _rows_bf16(words):
    return pltpu.bitcast(jnp.broadcast_to(words, (PEER_KEYS // 2, LANES)), BF16)


def _peer_dense_kernel(u_ref, vt_ref, xnt_ref, e2_ref, r2_ref, n1_ref, w1_ref, yt_ref, h_ref):
    e = pl.program_id(1)
    t = xnt_ref.shape[1]

    @pl.when(e == 0)
    def _():
        yt_ref[...] = jnp.zeros_like(yt_ref)

    at = jnp.dot(u_ref[...], xnt_ref[...], preferred_element_type=F32)
    for rl in range(SUBLANES):
        for tc in range(t // LANES):
            cols = slice(tc * LANES, (tc + 1) * LANES)
            gate = jnp.zeros((PEER_KEYS, LANES), BF16)
            for h in range(PEER_HEADS):
                n1r = _rows_bf16(n1_ref[h, e, rl:rl + 1, cols])
                w1r = _rows_bf16(w1_ref[h, e, rl:rl + 1, cols])
                r2 = pltpu.bitcast(r2_ref[h, :, cols], BF16)
                e2 = pltpu.bitcast(e2_ref[h, :, cols], BF16)
                gate = gate + jnp.where(r2 < n1r, e2 * w1r, jnp.zeros((), BF16))
            a = at[rl * PEER_KEYS:(rl + 1) * PEER_KEYS, cols]
            h_ref[rl * PEER_KEYS:(rl + 1) * PEER_KEYS, cols] = jax.nn.gelu(a.astype(BF16)) * gate
    yt_ref[...] += jnp.dot(vt_ref[...], h_ref[...], preferred_element_type=F32)


def _peer_dense(u, vt, xnt, e2, r2, n1, w1):
    n_exp, d = u.shape
    n = xnt.shape[1]
    t = min(PEER_T, n)
    plane_spec = pl.BlockSpec((PEER_HEADS, PEER_KEYS // 2, t), lambda i, e: (0, 0, i))
    table_spec = pl.BlockSpec((PEER_HEADS, PEER_KEYS // SUBLANES, SUBLANES, t),
                              lambda i, e: (0, 0, 0, i))
    return pl.pallas_call(
        _peer_dense_kernel,
        grid=(n // t, n_exp // PEER_ET),
        in_specs=[pl.BlockSpec((PEER_ET, d), lambda i, e: (e, 0)),
                  pl.BlockSpec((d, PEER_ET), lambda i, e: (0, e)),
                  pl.BlockSpec((d, t), lambda i, e: (0, i)),
                  plane_spec, plane_spec, table_spec, table_spec],
        out_specs=pl.BlockSpec((d, t), lambda i, e: (0, i)),
        out_shape=jax.ShapeDtypeStruct((d, n), F32),
        scratch_shapes=[pltpu.VMEM((PEER_ET, t), BF16)],
        compiler_params=_cparams(("parallel", "arbitrary")),
        name="peer_dense",
    )(u, vt, xnt, e2, r2, n1, w1)


def _add_t_kernel(x_ref, yt_ref, o_ref):
    o_ref[...] = x_ref[...] + yt_ref[...].T


def _add_t_norm_kernel(x_ref, yt_ref, g_ref, o_ref):
    o_ref[...] = _rms(x_ref[...] + yt_ref[...].T, g_ref[...])


def _add_transposed(x, yt, gain=None, tm=512):
    n, d = x.shape
    tm = min(tm, n)
    in_specs = [pl.BlockSpec((tm, d), lambda i: (i, 0)),
                pl.BlockSpec((d, tm), lambda i: (0, i))]
    args = [x, yt]
    body = _add_t_kernel
    if gain is not None:
        in_specs.append(pl.BlockSpec((1, d), lambda i: (0, 0)))
        args.append(gain.reshape(1, d))
        body = _add_t_norm_kernel
    return pl.pallas_call(
        body,
        grid=(n // tm,),
        in_specs=in_specs,
        out_specs=pl.BlockSpec((tm, d), lambda i: (i, 0)),
        out_shape=jax.ShapeDtypeStruct((n, d), F32),
        compiler_params=_cparams(("parallel",)),
        name="add_transposed",
    )(*args)


def _swap_halves(w):
    half = w.shape[-1] // 2
    return jnp.concatenate([w[..., half:], w[..., :half]], axis=-1)


def _prep_w_in(w_in):
    d = w_in.shape[0]
    kr = w_in[:, KR_OFF:D_IN]
    z = jnp.zeros((d, HEAD_DIM - C_ROPE), w_in.dtype)
    return jnp.concatenate([w_in[:, :KR_OFF], kr, z, _swap_halves(kr), z], axis=-1).astype(BF16)


def _prep_w_uq(w_uq):
    r = w_uq.shape[0]
    z = jnp.zeros((r, HEAD_DIM - C_ROPE), w_uq.dtype)
    blocks = []
    for h in range(C_HEADS):
        b = h * (C_NOPE + C_ROPE)
        rope = w_uq[:, b + C_NOPE:b + C_NOPE + C_ROPE]
        blocks += [w_uq[:, b:b + C_NOPE], rope, z, _swap_halves(rope), z]
    return jnp.concatenate(blocks, axis=-1).astype(BF16)


def _prep_w_ukv(w_ukv):
    ks = [w_ukv[:, h * (C_NOPE + C_V):h * (C_NOPE + C_V) + C_NOPE] for h in range(C_HEADS)]
    vs = [w_ukv[:, h * (C_NOPE + C_V) + C_NOPE:(h + 1) * (C_NOPE + C_V)] for h in range(C_HEADS)]
    return jnp.concatenate(ks + vs, axis=-1).astype(BF16)


def _mixer(x2, bsz, s_len, ln1, w_in, a_sink, b_rel_bias, c_q_norm, c_kv_norm, c_w_uq, c_w_ukv,
           out_norm, w_o):
    h2 = _norm_matmul(x2, ln1, _prep_w_in(w_in))
    h3 = h2.reshape(bsz, s_len, D_IN_PAD)
    oa = _attn_a(h3, a_sink.astype(F32), out_norm[:A_Q])
    ob = _attn_b(h3, b_rel_bias, out_norm[A_Q:A_Q + B_W])
    qc, kc, vc = _mla_proj(h2, s_len, c_q_norm, c_kv_norm, _prep_w_uq(c_w_uq), _prep_w_ukv(c_w_ukv))
    oc = _mla_attn(qc.reshape(bsz, s_len, -1), kc.reshape(bsz, s_len, -1),
                   vc.reshape(bsz, s_len, -1), out_norm[A_Q + B_W:])
    n = bsz * s_len
    return _out_proj(x2, oa.reshape(n, A_Q), ob.reshape(n, B_W), oc.reshape(n, C_OUT),
                     w_o.astype(BF16))


def _peer(x2, ln2, w_q, sub_keys, u, vv):
    keys = sub_keys.astype(F32)
    keys = jnp.stack([keys[0], jnp.concatenate([keys[1, 0::2], keys[1, 1::2]], axis=0)])
    xnt, e2, r2, n1, w1 = _peer_route(x2, ln2, w_q.T.astype(BF16), keys)
    return _peer_dense(u.astype(BF16), vv.T.astype(BF16), xnt, e2, r2, n1, w1)


def kernel(x, ln1, w_in, a_sink, b_rel_bias, c_q_norm, c_kv_norm, c_w_uq, c_w_ukv, out_norm, w_o,
           ln2, peer_w_q, peer_sub_keys, peer_u, peer_v, final_norm):
    bsz, s_len, d = x.shape
    depth = ln1.shape[0]
    x2 = x.reshape(bsz * s_len, d)
    for l in range(depth):
        x2 = _mixer(x2, bsz, s_len, ln1[l], w_in[l], a_sink[l], b_rel_bias[l], c_q_norm[l],
                    c_kv_norm[l], c_w_uq[l], c_w_ukv[l], out_norm[l], w_o[l])
        yt = _peer(x2, ln2[l], peer_w_q[l], peer_sub_keys[l], peer_u[l], peer_v[l])
        x2 = _add_transposed(x2, yt, final_norm if l == depth - 1 else None)
    return x2.reshape(bsz, s_len, d)
```

```python
import functools
import math

import jax
import jax.numpy as jnp
import numpy as np
from jax import lax
from jax.experimental import pallas as pl
from jax.experimental.pallas import tpu as pltpu

F32 = jnp.float32
BF16 = jnp.bfloat16

EPS = 1e-6
HEAD_DIM = 128
A_HEADS = 8
A_KV_HEADS = 2
A_GROUP = A_HEADS // A_KV_HEADS
A_WINDOW = 128
B_HEADS = 4
GRID_W = 64
NB_ROWS = 8
NB_COLS = 16
C_HEADS = 4
C_Q_RANK = 512
C_KV_RANK = 256
C_NOPE = 128
C_ROPE = 64
C_V = 128
ROPE_THETA = 10000.0
A_Q = A_HEADS * HEAD_DIM
A_KV = A_KV_HEADS * HEAD_DIM
B_W = B_HEADS * HEAD_DIM
C_OUT = C_HEADS * C_V
D_MIX = A_Q + B_W + C_OUT
D_IN = A_Q + 2 * A_KV + 3 * B_W + C_Q_RANK + C_KV_RANK + C_ROPE
D_IN_PAD = 4096
KR_OFF = D_IN - C_ROPE
PEER_HEADS = 8
PEER_KEYS = 128
PEER_HALF = 64
PEER_TOPK = 16

VMEM_LIMIT = 56 * 1024 * 1024
NEG_INF = float("-inf")


def _cparams(sem):
    return pltpu.CompilerParams(dimension_semantics=sem, vmem_limit_bytes=VMEM_LIMIT)


def _rms(x, g):
    ms = jnp.mean(x * x, axis=-1, keepdims=True)
    return x * lax.rsqrt(ms + EPS) * g


def _norm_matmul_kernel(x_ref, g_ref, w_ref, o_ref, xn_ref):
    @pl.when(pl.program_id(1) == 0)
    def _():
        xn_ref[...] = _rms(x_ref[...], g_ref[...]).astype(BF16)

    o_ref[...] = jnp.dot(xn_ref[...], w_ref[...],
                         preferred_element_type=F32).astype(o_ref.dtype)


def _norm_matmul(x, g, w, tm=1024, tn=512):
    n, d = x.shape
    nout = w.shape[1]
    tm = min(tm, n)
    return pl.pallas_call(
        _norm_matmul_kernel,
        grid=(n // tm, nout // tn),
        in_specs=[pl.BlockSpec((tm, d), lambda i, j: (i, 0)),
                  pl.BlockSpec((1, d), lambda i, j: (0, 0)),
                  pl.BlockSpec((d, tn), lambda i, j: (0, j))],
        out_specs=pl.BlockSpec((tm, tn), lambda i, j: (i, j)),
        out_shape=jax.ShapeDtypeStruct((n, nout), BF16),
        scratch_shapes=[pltpu.VMEM((tm, d), BF16)],
        compiler_params=_cparams(("parallel", "arbitrary")),
        name="norm_matmul",
    )(x, g.reshape(1, d), w)


def _add_norm_matmul_kernel(x_ref, yt_ref, g_ref, w_ref, o_ref, xo_ref, xn_ref):
    @pl.when(pl.program_id(1) == 0)
    def _():
        x = x_ref[...] + yt_ref[...].T
        xo_ref[...] = x
        xn_ref[...] = _rms(x, g_ref[...]).astype(BF16)

    o_ref[...] = jnp.dot(xn_ref[...], w_ref[...],
                         preferred_element_type=F32).astype(o_ref.dtype)


def _add_norm_matmul(x, yt, g, w, tm=512, tn=512):
    n, d = x.shape
    nout = w.shape[1]
    tm = min(tm, n)
    return pl.pallas_call(
        _add_norm_matmul_kernel,
        grid=(n // tm, nout // tn),
        in_specs=[pl.BlockSpec((tm, d), lambda i, j: (i, 0)),
                  pl.BlockSpec((d, tm), lambda i, j: (0, i)),
                  pl.BlockSpec((1, d), lambda i, j: (0, 0)),
                  pl.BlockSpec((d, tn), lambda i, j: (0, j))],
        out_specs=[pl.BlockSpec((tm, tn), lambda i, j: (i, j)),
                   pl.BlockSpec((tm, d), lambda i, j: (i, 0))],
        out_shape=[jax.ShapeDtypeStruct((n, nout), BF16), jax.ShapeDtypeStruct((n, d), F32)],
        scratch_shapes=[pltpu.VMEM((tm, d), BF16)],
        compiler_params=_cparams(("parallel", "arbitrary")),
        name="add_norm_matmul",
    )(x, yt, g.reshape(1, d), w)


A_TQ = 512
A_BAND = 3 * A_WINDOW


def _attn_a_kernel(sink_ref, q_ref, k_ref, v_ref, g_ref, o_ref):
    i = pl.program_id(1)
    s_len = k_ref.shape[0]
    scale = HEAD_DIM ** -0.5
    slopes = [2.0 ** (-8.0 * (h + 1) / A_HEADS) for h in range(A_HEADS)]
    for j in range(A_TQ // A_WINDOW):
        q0 = i * A_TQ + j * A_WINDOW
        start = pl.multiple_of(jnp.clip(q0 - A_WINDOW, 0, s_len - A_BAND), A_WINDOW)
        kb = k_ref[pl.ds(start, A_BAND), :]
        vb = v_ref[pl.ds(start, A_BAND), :]
        q_pos = q0 + lax.broadcasted_iota(jnp.int32, (A_WINDOW, A_BAND), 0)
        k_pos = start + lax.broadcasted_iota(jnp.int32, (A_WINDOW, A_BAND), 1)
        dist = jnp.abs(q_pos - k_pos)
        valid = dist <= A_WINDOW
        dist_f = dist.astype(F32)
        outs = []
        ssq = jnp.zeros((A_WINDOW, 1), F32)
        for kh in range(A_KV_HEADS):
            kk = kb[:, kh * HEAD_DIM:(kh + 1) * HEAD_DIM]
            vv = vb[:, kh * HEAD_DIM:(kh + 1) * HEAD_DIM]
            for g in range(A_GROUP):
                h = kh * A_GROUP + g
                qh = q_ref[j * A_WINDOW:(j + 1) * A_WINDOW, h * HEAD_DIM:(h + 1) * HEAD_DIM]
                s = lax.dot_general(qh, kk, (((1,), (1,)), ((), ())),
                                    preferred_element_type=F32) * scale
                s = s - slopes[h] * dist_f
                s = jnp.where(valid, s, NEG_INF)
                sink = sink_ref[h]
                m = jnp.maximum(jnp.max(s, axis=-1, keepdims=True), sink)
                p = jnp.exp(s - m)
                denom = jnp.sum(p, axis=-1, keepdims=True) + jnp.exp(sink - m)
                o = jnp.dot(p.astype(BF16), vv, preferred_element_type=F32) * (1.0 / denom)
                ssq = ssq + jnp.sum(o * o, axis=-1, keepdims=True)
                outs.append(o)
        inv = lax.rsqrt(ssq / A_Q + EPS)
        for h in range(A_HEADS):
            cols = slice(h * HEAD_DIM, (h + 1) * HEAD_DIM)
            o_ref[j * A_WINDOW:(j + 1) * A_WINDOW, cols] = (
                outs[h] * inv * g_ref[:, cols]).astype(o_ref.dtype)


def _attn_a(h3, sink, gain):
    bsz, s_len, _ = h3.shape
    assert s_len % A_TQ == 0 and s_len >= A_BAND
    kcol = A_Q // A_KV
    return pl.pallas_call(
        _attn_a_kernel,
        grid=(bsz, s_len // A_TQ),
        in_specs=[pl.BlockSpec(memory_space=pltpu.SMEM),
                  pl.BlockSpec((None, A_TQ, A_Q), lambda b, i: (b, i, 0)),
                  pl.BlockSpec((None, s_len, A_KV), lambda b, i: (b, 0, kcol)),
                  pl.BlockSpec((None, s_len, A_KV), lambda b, i: (b, 0, kcol + 1)),
                  pl.BlockSpec((1, A_Q), lambda b, i: (0, 0))],
        out_specs=pl.BlockSpec((None, A_TQ, A_Q), lambda b, i: (b, i, 0)),
        out_shape=jax.ShapeDtypeStruct((bsz, s_len, A_Q), BF16),
        compiler_params=_cparams(("parallel", "arbitrary")),
        name="attn_a",
    )(sink, h3, h3, h3, gain.reshape(1, A_Q))


B_RQ = 8
B_KEYS = NB_ROWS * GRID_W
B_QCOL = (A_Q + 2 * A_KV) // B_W


def _attn_b_kernel(q_ref, k_ref, v_ref, bias_ref, g_ref, o_ref):
    i = pl.program_id(1)
    rows = k_ref.shape[0] // GRID_W
    scale = HEAD_DIM ** -0.5
    for jr in range(B_RQ):
        r = i * B_RQ + jr
        r_start = jnp.clip(r - NB_ROWS // 2, 0, rows - NB_ROWS)
        off = r_start - r + NB_ROWS - 1
        kstart = pl.multiple_of(r_start * GRID_W, GRID_W)
        kb = k_ref[pl.ds(kstart, B_KEYS), :]
        vb = v_ref[pl.ds(kstart, B_KEYS), :]
        hs = [slice(h * HEAD_DIM, (h + 1) * HEAD_DIM) for h in range(B_HEADS)]
        q3 = jnp.stack([q_ref[jr * GRID_W:(jr + 1) * GRID_W, c] for c in hs])
        k3 = jnp.stack([kb[:, c] for c in hs])
        v3 = jnp.stack([vb[:, c] for c in hs])
        s = jnp.einsum('hqd,hkd->hqk', q3, k3, preferred_element_type=F32) * scale + bias_ref[off]
        m = jnp.max(s, axis=-1, keepdims=True)
        p = jnp.exp(s - m)
        inv_l = 1.0 / jnp.sum(p, axis=-1, keepdims=True)
        o = jnp.einsum('hqk,hkd->hqd', p.astype(BF16), v3, preferred_element_type=F32) * inv_l
        ssq = jnp.sum(jnp.sum(o * o, axis=-1, keepdims=True), axis=0)
        outs = [o[h] for h in range(B_HEADS)]
        inv = lax.rsqrt(ssq / B_W + EPS)
        for h in range(B_HEADS):
            cols = slice(h * HEAD_DIM, (h + 1) * HEAD_DIM)
            o_ref[jr * GRID_W:(jr + 1) * GRID_W, cols] = (
                outs[h] * inv * g_ref[:, cols]).astype(o_ref.dtype)


def _nb_bias_table(rel_bias):
    col = jnp.arange(GRID_W)
    c_start = jnp.clip(col - NB_COLS // 2, 0, GRID_W - NB_COLS)
    col_ok = (col[None, :] >= c_start[:, None]) & (col[None, :] < c_start[:, None] + NB_COLS)
    dc_idx = jnp.clip(col[None, :] - col[:, None] + NB_COLS - 1, 0, 2 * NB_COLS - 2)
    rb = rel_bias.astype(F32)[:, :, dc_idx]
    rb = jnp.where(col_ok[None, None], rb, NEG_INF)
    tabs = []
    for off in range(NB_ROWS):
        t = jnp.transpose(rb[:, off:off + NB_ROWS], (0, 2, 1, 3))
        tabs.append(t.reshape(B_HEADS, GRID_W, B_KEYS))
    return jnp.stack(tabs)


def _attn_b(h3, rel_bias, gain):
    bsz, s_len, _ = h3.shape
    rows = s_len // GRID_W
    assert rows >= NB_ROWS and rows % B_RQ == 0
    tq = B_RQ * GRID_W
    bias = _nb_bias_table(rel_bias)
    return pl.pallas_call(
        _attn_b_kernel,
        grid=(bsz, rows // B_RQ),
        in_specs=[pl.BlockSpec((None, tq, B_W), lambda b, i: (b, i, B_QCOL)),
                  pl.BlockSpec((None, s_len, B_W), lambda b, i: (b, 0, B_QCOL + 1)),
                  pl.BlockSpec((None, s_len, B_W), lambda b, i: (b, 0, B_QCOL + 2)),
                  pl.BlockSpec((NB_ROWS, B_HEADS, GRID_W, B_KEYS), lambda b, i: (0, 0, 0, 0)),
                  pl.BlockSpec((1, B_W), lambda b, i: (0, 0))],
        out_specs=pl.BlockSpec((None, tq, B_W), lambda b, i: (b, i, 0)),
        out_shape=jax.ShapeDtypeStruct((bsz, s_len, B_W), BF16),
        compiler_params=_cparams(("parallel", "arbitrary")),
        name="attn_b",
    )(h3, h3, h3, bias, gain.reshape(1, B_W))


C_QK = 256
C_QW = 3 * HEAD_DIM
MLA_TM = 512
MLA_TQ = 256


def _mla_proj_kernel(cq_ref, ckv_ref, kr_ref, gq_ref, gkv_ref, wq_ref, wkv_ref,
                     cos_ref, sin_ref, q_ref, k_ref, v_ref):
    cos = cos_ref[...]
    sin = sin_ref[...]
    cqn = _rms(cq_ref[...].astype(F32), gq_ref[...]).astype(BF16)
    qf = jnp.dot(cqn, wq_ref[...], preferred_element_type=F32) * ((C_NOPE + C_ROPE) ** -0.5)
    ckvn = _rms(ckv_ref[...].astype(F32), gkv_ref[...]).astype(BF16)
    kvf = jnp.dot(ckvn, wkv_ref[...], preferred_element_type=F32)
    kr = kr_ref[...].astype(F32)
    kpe = (kr[:, :HEAD_DIM] * cos + kr[:, HEAD_DIM:] * sin).astype(k_ref.dtype)
    for h in range(C_HEADS):
        b = h * C_QW
        q_ref[:, h * C_QK:h * C_QK + C_NOPE] = qf[:, b:b + C_NOPE].astype(q_ref.dtype)
        rope = qf[:, b + HEAD_DIM:b + 2 * HEAD_DIM] * cos + qf[:, b + 2 * HEAD_DIM:b + 3 * HEAD_DIM] * sin
        q_ref[:, h * C_QK + C_NOPE:(h + 1) * C_QK] = rope.astype(q_ref.dtype)
        k_ref[:, h * C_QK:h * C_QK + C_NOPE] = kvf[:, h * C_NOPE:(h + 1) * C_NOPE].astype(k_ref.dtype)
        k_ref[:, h * C_QK + C_NOPE:(h + 1) * C_QK] = kpe
    v_ref[...] = kvf[:, C_HEADS * C_NOPE:].astype(v_ref.dtype)


def _rope_tables(s_len):
    half = C_ROPE // 2
    inv = ROPE_THETA ** (-jnp.arange(half, dtype=F32) / half)
    ang = jnp.arange(s_len, dtype=F32)[:, None] * inv[None, :]
    cos = jnp.cos(ang)
    sin = jnp.sin(ang)
    zeros = jnp.zeros((s_len, HEAD_DIM - C_ROPE), F32)
    return (jnp.concatenate([cos, cos, zeros], axis=-1),
            jnp.concatenate([-sin, sin, zeros], axis=-1))


def _mla_proj(h2, s_len, gq, gkv, wq, wkv):
    n = h2.shape[0]
    tm = min(MLA_TM, s_len)
    cos, sin = _rope_tables(s_len)
    per_seq = s_len // tm
    qk_shape = jax.ShapeDtypeStruct((n, C_HEADS * C_QK), BF16)
    cq_col = (D_IN - C_ROPE - C_KV_RANK - C_Q_RANK) // C_Q_RANK
    ckv_col = (D_IN - C_ROPE - C_KV_RANK) // C_KV_RANK
    kr_col = KR_OFF // (2 * HEAD_DIM)
    return pl.pallas_call(
        _mla_proj_kernel,
        grid=(n // tm,),
        in_specs=[pl.BlockSpec((tm, C_Q_RANK), lambda i: (i, cq_col)),
                  pl.BlockSpec((tm, C_KV_RANK), lambda i: (i, ckv_col)),
                  pl.BlockSpec((tm, 2 * HEAD_DIM), lambda i: (i, kr_col)),
                  pl.BlockSpec((1, C_Q_RANK), lambda i: (0, 0)),
                  pl.BlockSpec((1, C_KV_RANK), lambda i: (0, 0)),
                  pl.BlockSpec(wq.shape, lambda i: (0, 0)),
                  pl.BlockSpec(wkv.shape, lambda i: (0, 0)),
                  pl.BlockSpec((tm, HEAD_DIM), lambda i: (i % per_seq, 0)),
                  pl.BlockSpec((tm, HEAD_DIM), lambda i: (i % per_seq, 0))],
        out_specs=[pl.BlockSpec((tm, C_HEADS * C_QK), lambda i: (i, 0)),
                   pl.BlockSpec((tm, C_HEADS * C_QK), lambda i: (i, 0)),
                   pl.BlockSpec((tm, C_OUT), lambda i: (i, 0))],
        out_shape=[qk_shape, qk_shape, jax.ShapeDtypeStruct((n, C_OUT), BF16)],
        compiler_params=_cparams(("parallel",)),
        name="mla_proj",
    )(h2, h2, h2, gq.reshape(1, -1), gkv.reshape(1, -1), wq, wkv, cos, sin)


def _mla_attn_kernel(q_ref, k_ref, v_ref, g_ref, o_ref):
    outs = []
    ssq = jnp.zeros((q_ref.shape[0], 1), F32)
    for h in range(C_HEADS):
        qk = slice(h * C_QK, (h + 1) * C_QK)
        s = lax.dot_general(q_ref[:, qk], k_ref[:, qk], (((1,), (1,)), ((), ())),
                            preferred_element_type=F32)
        m = jnp.max(s, axis=-1, keepdims=True)
        p = jnp.exp(s - m)
        inv_l = 1.0 / jnp.sum(p, axis=-1, keepdims=True)
        o = jnp.dot(p.astype(BF16), v_ref[:, h * C_V:(h + 1) * C_V],
                    preferred_element_type=F32) * inv_l
        ssq = ssq + jnp.sum(o * o, axis=-1, keepdims=True)
        outs.append(o)
    inv = lax.rsqrt(ssq / C_OUT + EPS)
    for h in range(C_HEADS):
        cols = slice(h * C_V, (h + 1) * C_V)
        o_ref[:, cols] = (outs[h] * inv * g_ref[:, cols]).astype(o_ref.dtype)


def _mla_attn(q3, k3, v3, gain):
    bsz, s_len, _ = q3.shape
    tq = min(MLA_TQ, s_len)
    return pl.pallas_call(
        _mla_attn_kernel,
        grid=(bsz, s_len // tq),
        in_specs=[pl.BlockSpec((None, tq, C_HEADS * C_QK), lambda b, i: (b, i, 0)),
                  pl.BlockSpec((None, s_len, C_HEADS * C_QK), lambda b, i: (b, 0, 0)),
                  pl.BlockSpec((None, s_len, C_OUT), lambda b, i: (b, 0, 0)),
                  pl.BlockSpec((1, C_OUT), lambda b, i: (0, 0))],
        out_specs=pl.BlockSpec((None, tq, C_OUT), lambda b, i: (b, i, 0)),
        out_shape=jax.ShapeDtypeStruct((bsz, s_len, C_OUT), BF16),
        compiler_params=_cparams(("parallel", "arbitrary")),
        name="mla_attn",
    )(q3, k3, v3, gain.reshape(1, C_OUT))


def _out_proj_kernel(x_ref, oa_ref, ob_ref, oc_ref, wa_ref, wb_ref, wc_ref, o_ref):
    acc = jnp.dot(oa_ref[...], wa_ref[...], preferred_element_type=F32)
    acc = acc + jnp.dot(ob_ref[...], wb_ref[...], preferred_element_type=F32)
    acc = acc + jnp.dot(oc_ref[...], wc_ref[...], preferred_element_type=F32)
    o_ref[...] = x_ref[...] + acc


def _out_proj(x, oa, ob, oc, w_o, tm=1024, tn=512):
    n, d = x.shape
    tm = min(tm, n)
    return pl.pallas_call(
        _out_proj_kernel,
        grid=(n // tm, d // tn),
        in_specs=[pl.BlockSpec((tm, tn), lambda i, j: (i, j)),
                  pl.BlockSpec((tm, A_Q), lambda i, j: (i, 0)),
                  pl.BlockSpec((tm, B_W), lambda i, j: (i, 0)),
                  pl.BlockSpec((tm, C_OUT), lambda i, j: (i, 0)),
                  pl.BlockSpec((A_Q, tn), lambda i, j: (0, j)),
                  pl.BlockSpec((B_W, tn), lambda i, j: (A_Q // B_W, j)),
                  pl.BlockSpec((C_OUT, tn), lambda i, j: ((A_Q + B_W) // C_OUT, j))],
        out_specs=pl.BlockSpec((tm, tn), lambda i, j: (i, j)),
        out_shape=jax.ShapeDtypeStruct((n, d), F32),
        compiler_params=_cparams(("parallel", "arbitrary")),
        name="out_proj",
    )(x, oa, ob, oc, w_o, w_o, w_o)


ROUTE_TM = 256
_CAND_ROWS = [(a, PEER_TOPK // (a + 1)) for a in range(PEER_TOPK)]
SUBLANES = 8


def _top_ranks(s, idx):
    rank = jnp.full(s.shape, float(PEER_KEYS), F32)
    vals = []
    for k in range(PEER_TOPK):
        m = jnp.max(s, axis=0, keepdims=True)
        first = jnp.min(jnp.where(s == m, idx, float(PEER_KEYS)), axis=0, keepdims=True)
        hit = idx == first
        rank = jnp.where(hit, float(k), rank)
        s = jnp.where(hit, NEG_INF, s)
        vals.append(m)
    return rank, vals


def _bf16_bits_high(x):
    return pltpu.bitcast(x.astype(BF16).astype(F32), jnp.uint32)


def _pack_halves(x):
    half = x.shape[0] // 2
    return (_bf16_bits_high(x[:half]) >> 16) | _bf16_bits_high(x[half:])


def _pack_twice(x):
    bits = _bf16_bits_high(x)
    return (bits >> 16) | bits


def _peer_route_kernel(x_ref, g_ref, wq_ref, keys_ref, xnt_ref, e2_ref, r2_ref, n1_ref, w1_ref):
    tm = x_ref.shape[0]
    xn = _rms(x_ref[...], g_ref[...])
    xnt = xn.T.astype(BF16)
    xnt_ref[...] = xnt
    qt = jnp.dot(wq_ref[...], xnt, preferred_element_type=F32)
    row = lax.broadcasted_iota(jnp.int32, (PEER_KEYS, tm), 0).astype(F32)
    row2 = jnp.where(row < PEER_KEYS // 2, 2.0 * row, 2.0 * row - (PEER_KEYS - 1))

    def route_head(h):
        sc = []
        for c in range(2):
            row0 = (h * 2 + c) * PEER_HALF
            sc.append(jnp.dot(keys_ref[c], qt[row0:row0 + PEER_HALF, :],
                              preferred_element_type=F32,
                              precision=lax.Precision.HIGHEST))
        rank1, t1 = _top_ranks(sc[0], row)
        rank2, t2 = _top_ranks(sc[1], row2)
        e1 = [jnp.exp(t1[a] - t1[0]) for a in range(PEER_TOPK)]
        e2 = [jnp.exp(t2[b] - t2[0]) for b in range(PEER_TOPK)]
        groups, gmeta = [], []
        for a, nb in _CAND_ROWS:
            for b0 in range(0, nb, SUBLANES):
                bs = list(range(b0, min(b0 + SUBLANES, nb)))
                rows = [t1[a] + t2[b] for b in bs]
                rows += [jnp.full((1, tm), NEG_INF, F32)] * (SUBLANES - len(bs))
                groups.append(jnp.concatenate(rows, axis=0))
                gmeta.append((a, bs))
        cand = jnp.concatenate(groups, axis=0)
        cidx = lax.broadcasted_iota(jnp.int32, cand.shape, 0).astype(F32)
        crank, _ = _top_ranks(cand, cidx)
        self_f = jnp.where(crank < float(PEER_TOPK), 1.0, 0.0)
        n_a = [jnp.zeros((1, tm), F32) for _ in range(PEER_TOPK)]
        z = jnp.zeros((1, tm), F32)
        for gi, (a, bs) in enumerate(gmeta):
            blk = self_f[gi * SUBLANES:(gi + 1) * SUBLANES, :]
            n_a[a] = n_a[a] + jnp.sum(blk, axis=0, keepdims=True)
            for bi, b in enumerate(bs):
                z = z + blk[bi:bi + 1, :] * (e1[a] * e2[b])
        n1 = jnp.zeros(rank1.shape, F32)
        for a in range(PEER_TOPK):
            n1 = jnp.where(rank1 == float(a), n_a[a], n1)
        e2_ref[h] = _pack_halves(jnp.exp(sc[1] - t2[0]))
        r2_ref[h] = _pack_halves(rank2)
        n1_ref[h] = _pack_twice(n1).reshape(PEER_KEYS // SUBLANES, SUBLANES, tm)
        w1_ref[h] = _pack_twice(jnp.exp(sc[0] - t1[0]) / z).reshape(PEER_KEYS // SUBLANES, SUBLANES, tm)

    for h in range(PEER_HEADS):
        route_head(h)


def _peer_route(x, g, wq_t, sub_keys):
    n, d = x.shape
    tm = min(ROUTE_TM, n)
    plane = jax.ShapeDtypeStruct((PEER_HEADS, PEER_KEYS // 2, n), jnp.uint32)
    plane_spec = pl.BlockSpec((PEER_HEADS, PEER_KEYS // 2, tm), lambda i: (0, 0, i))
    table = jax.ShapeDtypeStruct((PEER_HEADS, PEER_KEYS // SUBLANES, SUBLANES, n), jnp.uint32)
    table_spec = pl.BlockSpec((PEER_HEADS, PEER_KEYS // SUBLANES, SUBLANES, tm),
                              lambda i: (0, 0, 0, i))
    return pl.pallas_call(
        _peer_route_kernel,
        grid=(n // tm,),
        in_specs=[pl.BlockSpec((tm, d), lambda i: (i, 0)),
                  pl.BlockSpec((1, d), lambda i: (0, 0)),
                  pl.BlockSpec(wq_t.shape, lambda i: (0, 0)),
                  pl.BlockSpec(sub_keys.shape, lambda i: (0, 0, 0))],
        out_specs=[pl.BlockSpec((d, tm), lambda i: (0, i)),
                   plane_spec, plane_spec, table_spec, table_spec],
        out_shape=[jax.ShapeDtypeStruct((d, n), BF16), plane, plane, table, table],
        compiler_params=_cparams(("parallel",)),
        name="peer_route",
    )(x, g.reshape(1, d), wq_t, sub_keys)


PEER_T = 512
PEER_ET = SUBLANES * PEER_KEYS
LANES = 128


def _rows_bf16(words):
    return pltpu.bitcast(jnp.broadcast_to(words, (PEER_KEYS // 2, LANES)), BF16)


def _peer_dense_kernel(u_ref, vt_ref, xnt_ref, e2_ref, r2_ref, n1_ref, w1_ref, yt_ref, h_ref):
    e = pl.program_id(1)
    t = xnt_ref.shape[1]

    @pl.when(e == 0)
    def _():
        yt_ref[...] = jnp.zeros_like(yt_ref)

    at = jnp.dot(u_ref[...], xnt_ref[...], preferred_element_type=F32)
    for rl in range(SUBLANES):
        for tc in range(t // LANES):
            cols = slice(tc * LANES, (tc + 1) * LANES)
            gate = jnp.zeros((PEER_KEYS, LANES), BF16)
            for h in range(PEER_HEADS):
                n1r = _rows_bf16(n1_ref[h, e, rl:rl + 1, cols])
                w1r = _rows_bf16(w1_ref[h, e, rl:rl + 1, cols])
                r2 = pltpu.bitcast(r2_ref[h, :, cols], BF16)
                e2 = pltpu.bitcast(e2_ref[h, :, cols], BF16)
                gate = gate + jnp.where(r2 < n1r, e2 * w1r, jnp.zeros((), BF16))
            a = at[rl * PEER_KEYS:(rl + 1) * PEER_KEYS, cols]
            h_ref[rl * PEER_KEYS:(rl + 1) * PEER_KEYS, cols] = jax.nn.gelu(a.astype(BF16)) * gate
    yt_ref[...] += jnp.dot(vt_ref[...], h_ref[...], preferred_element_type=F32)


def _peer_dense(u, vt, xnt, e2, r2, n1, w1):
    n_exp, d = u.shape
    n = xnt.shape[1]
    t = min(PEER_T, n)
    plane_spec = pl.BlockSpec((PEER_HEADS, PEER_KEYS // 2, t), lambda i, e: (0, 0, i))
    table_spec = pl.BlockSpec((PEER_HEADS, PEER_KEYS // SUBLANES, SUBLANES, t),
                              lambda i, e: (0, 0, 0, i))
    return pl.pallas_call(
        _peer_dense_kernel,
        grid=(n // t, n_exp // PEER_ET),
        in_specs=[pl.BlockSpec((PEER_ET, d), lambda i, e: (e, 0)),
                  pl.BlockSpec((d, PEER_ET), lambda i, e: (0, e)),
                  pl.BlockSpec((d, t), lambda i, e: (0, i)),
                  plane_spec, plane_spec, table_spec, table_spec],
        out_specs=pl.BlockSpec((d, t), lambda i, e: (0, i)),
        out_shape=jax.ShapeDtypeStruct((d, n), F32),
        scratch_shapes=[pltpu.VMEM((PEER_ET, t), BF16)],
        compiler_params=_cparams(("parallel", "arbitrary")),
        name="peer_dense",
    )(u, vt, xnt, e2, r2, n1, w1)


def _add_t_kernel(x_ref, yt_ref, o_ref):
    o_ref[...] = x_ref[...] + yt_ref[...].T


def _add_t_norm_kernel(x_ref, yt_ref, g_ref, o_ref):
    o_ref[...] = _rms(x_ref[...] + yt_ref[...].T, g_ref[...])


def _add_transposed(x, yt, gain=None, tm=512):
    n, d = x.shape
    tm = min(tm, n)
    in_specs = [pl.BlockSpec((tm, d), lambda i: (i, 0)),
                pl.BlockSpec((d, tm), lambda i: (0, i))]
    args = [x, yt]
    body = _add_t_kernel
    if gain is not None:
        in_specs.append(pl.BlockSpec((1, d), lambda i: (0, 0)))
        args.append(gain.reshape(1, d))
        body = _add_t_norm_kernel
    return pl.pallas_call(
        body,
        grid=(n // tm,),
        in_specs=in_specs,
        out_specs=pl.BlockSpec((tm, d), lambda i: (i, 0)),
        out_shape=jax.ShapeDtypeStruct((n, d), F32),
        compiler_params=_cparams(("parallel",)),
        name="add_transposed",
    )(*args)


def _swap_halves(w):
    half = w.shape[-1] // 2
    return jnp.concatenate([w[..., half:], w[..., :half]], axis=-1)


def _prep_w_in(w_in):
    d = w_in.shape[0]
    kr = w_in[:, KR_OFF:D_IN]
    z = jnp.zeros((d, HEAD_DIM - C_ROPE), w_in.dtype)
    return jnp.concatenate([w_in[:, :KR_OFF], kr, z, _swap_halves(kr), z], axis=-1).astype(BF16)


def _prep_w_uq(w_uq):
    r = w_uq.shape[0]
    z = jnp.zeros((r, HEAD_DIM - C_ROPE), w_uq.dtype)
    blocks = []
    for h in range(C_HEADS):
        b = h * (C_NOPE + C_ROPE)
        rope = w_uq[:, b + C_NOPE:b + C_NOPE + C_ROPE]
        blocks += [w_uq[:, b:b + C_NOPE], rope, z, _swap_halves(rope), z]
    return jnp.concatenate(blocks, axis=-1).astype(BF16)


def _prep_w_ukv(w_ukv):
    ks = [w_ukv[:, h * (C_NOPE + C_V):h * (C_NOPE + C_V) + C_NOPE] for h in range(C_HEADS)]
    vs = [w_ukv[:, h * (C_NOPE + C_V) + C_NOPE:(h + 1) * (C_NOPE + C_V)] for h in range(C_HEADS)]
    return jnp.concatenate(ks + vs, axis=-1).astype(BF16)


def _mixer(x2, yt, bsz, s_len, ln1, w_in, a_sink, b_rel_bias, c_q_norm, c_kv_norm, c_w_uq, c_w_ukv,
           out_norm, w_o):
    if yt is None:
        h2 = _norm_matmul(x2, ln1, _prep_w_in(w_in))
    else:
        h2, x2 = _add_norm_matmul(x2, yt, ln1, _prep_w_in(w_in))
    h3 = h2.reshape(bsz, s_len, D_IN_PAD)
    oa = _attn_a(h3, a_sink.astype(F32), out_norm[:A_Q])
    ob = _attn_b(h3, b_rel_bias, out_norm[A_Q:A_Q + B_W])
    qc, kc, vc = _mla_proj(h2, s_len, c_q_norm, c_kv_norm, _prep_w_uq(c_w_uq), _prep_w_ukv(c_w_ukv))
    oc = _mla_attn(qc.reshape(bsz, s_len, -1), kc.reshape(bsz, s_len, -1),
                   vc.reshape(bsz, s_len, -1), out_norm[A_Q + B_W:])
    n = bsz * s_len
    return _out_proj(x2, oa.reshape(n, A_Q), ob.reshape(n, B_W), oc.reshape(n, C_OUT),
                     w_o.astype(BF16))


def _peer(x2, ln2, w_q, sub_keys, u, vv):
    keys = sub_keys.astype(F32)
    keys = jnp.stack([keys[0], jnp.concatenate([keys[1, 0::2], keys[1, 1::2]], axis=0)])
    xnt, e2, r2, n1, w1 = _peer_route(x2, ln2, w_q.T.astype(BF16), keys)
    return _peer_dense(u.astype(BF16), vv.T.astype(BF16), xnt, e2, r2, n1, w1)


def kernel(x, ln1, w_in, a_sink, b_rel_bias, c_q_norm, c_kv_norm, c_w_uq, c_w_ukv, out_norm, w_o,
           ln2, peer_w_q, peer_sub_keys, peer_u, peer_v, final_norm):
    bsz, s_len, d = x.shape
    depth = ln1.shape[0]
    x2 = x.reshape(bsz * s_len, d)
    yt = None
    for l in range(depth):
        x2 = _mixer(x2, yt, bsz, s_len, ln1[l], w_in[l], a_sink[l], b_rel_bias[l], c_q_norm[l],
                    c_kv_norm[l], c_w_uq[l], c_w_ukv[l], out_norm[l], w_o[l])
        yt = _peer(x2, ln2[l], peer_w_q[l], peer_sub_keys[l], peer_u[l], peer_v[l])
    return _add_transposed(x2, yt, final_norm).reshape(bsz, s_len, d)
```
